```python
import jax
import jax.numpy as jnp
from jax import lax
import numpy as np

D_MODEL = 2048
BATCH = 4
SEQ = 2048
DEPTH = 2
DEC_BATCH = 128
DEC_SEQ = 4
PAST_LEN = 16384
PAGE_SIZE = 128

MIX_WIDTH = D_MODEL // 2
H_A = 4
DV_A = MIX_WIDTH // H_A
DK_A = DV_A // 2
CHUNK_A = 64
GATE_CAP = 15.0
P_B = 64
H_B = MIX_WIDTH // P_B
G_B = 2
J_B = H_B // G_B
N_B = 128
CONV_W = 4
CONV_DIM = MIX_WIDTH + 2 * G_B * N_B
CHUNK_B = 128
K_C = 64
H_C = MIX_WIDTH // K_C
LORA_W = 96
LORA_A = 96
LORA_G = 256
R_W = 3 * MIX_WIDTH + LORA_W + LORA_A + LORA_G
A_COLS = 2 * H_A * DK_A + 2 * MIX_WIDTH + 2 * H_A
B_COLS = MIX_WIDTH + CONV_DIM + H_B
GATE_COLS = 3 * D_MODEL
P_TOT = A_COLS + B_COLS + R_W + GATE_COLS
D_FF = 5632
N_EXP = 8
TOP_K = 2
D_FF_E = 2816
N_DENSE = (DEPTH + 1) // 2
N_MOE = DEPTH // 2
EPS = 1e-6
LN_EPS_C = 64e-5

kernel_name = "hybrid_mlstm_ssd_rwkv7_step"


def _rmsnorm(x, g):
    xf = x.astype(jnp.float32)
    y = xf * lax.rsqrt(jnp.mean(xf * xf, -1, keepdims=True) + EPS)
    return (y * g.astype(jnp.float32)).astype(x.dtype)


def _split(u, sizes):
    outs, start = [], 0
    for s in sizes:
        outs.append(u[..., start:start + s])
        start += s
    return outs


def _pad_time(a, L, value=0.0):
    pad = (-a.shape[1]) % L
    if pad == 0:
        return a
    widths = [(0, 0)] * a.ndim
    widths[1] = (0, pad)
    return jnp.pad(a, widths, constant_values=value)


def _chunks(a, L):
    bt, t = a.shape[:2]
    return jnp.moveaxis(a.reshape((bt, t // L, L) + a.shape[2:]), 1, 0)


def _unchunk(a, t):
    a = jnp.moveaxis(a, 0, 1)
    return a.reshape((a.shape[0], -1) + a.shape[3:])[:, :t]


def _mlstm_chunked(q, k, v, log_i, log_f, C0, n0, m0):
    t = q.shape[1]
    L = min(CHUNK_A, t)
    q, k, v, log_f = [_chunks(_pad_time(a, L), L) for a in (q, k, v, log_f)]
    log_i = _chunks(_pad_time(log_i, L, -jnp.inf), L)
    causal = jnp.tril(jnp.ones((L, L), bool))[None, :, :, None]

    def body(carry, inp):
        C, n, m = carry
        qc, kc, vc, li, lf = inp
        b = jnp.cumsum(lf, axis=1)
        dmat = jnp.where(causal, b[:, :, None, :] - b[:, None, :, :] + li[:, None, :, :], -jnp.inf)
        g_inter = b + m[:, None, :]
        m_t = jnp.maximum(g_inter, dmat.max(axis=2))
        w_inter = jnp.exp(g_inter - m_t)
        s = jnp.einsum('bthk,bshk->btsh', qc, kc) * jnp.exp(dmat - m_t[:, :, None, :])
        num = w_inter[..., None] * jnp.einsum('bthk,bhkv->bthv', qc, C) + jnp.einsum('btsh,bshv->bthv', s, vc)
        den = w_inter * jnp.einsum('bthk,bhk->bth', qc, n) + s.sum(axis=2)
        h = num / jnp.maximum(jnp.abs(den), jnp.exp(-m_t))[..., None]
        b_last = b[:, -1]
        g_s = b_last[:, None] - b + li
        m_new = jnp.maximum(b_last + m, g_s.max(axis=1))
        a_s = jnp.exp(g_s - m_new[:, None])
        carry_decay = jnp.exp(b_last + m - m_new)
        C_new = carry_decay[..., None, None] * C + jnp.einsum('bshk,bshv->bhkv', a_s[..., None] * kc, vc)
        n_new = carry_decay[..., None] * n + jnp.einsum('bsh,bshk->bhk', a_s, kc)
        return (C_new, n_new, m_new), h

    (C, n, m), h = lax.scan(body, (C0, n0, m0), (q, k, v, log_i, log_f))
    return _unchunk(h, t), C, n, m


def _mlstm_branch(ua, C0, n0, m0, i_bias, f_bias, norm_g):
    bt, t, _ = ua.shape
    q, k, v, o, ig, fg = _split(ua, [H_A * DK_A, H_A * DK_A, MIX_WIDTH, MIX_WIDTH, H_A, H_A])
    q = q.reshape(bt, t, H_A, DK_A)
    k = k.reshape(bt, t, H_A, DK_A) * (DK_A ** -0.5)
    v = v.reshape(bt, t, H_A, DV_A)
    log_i = GATE_CAP * jnp.tanh((ig + i_bias) / GATE_CAP)
    log_f = jax.nn.log_sigmoid(GATE_CAP * jnp.tanh((fg + f_bias) / GATE_CAP))
    h, C, n, m = _mlstm_chunked(q, k, v, log_i, log_f, C0, n0, m0)
    h = h * lax.rsqrt(jnp.mean(h * h, -1, keepdims=True) + EPS) * norm_g.reshape(H_A, DV_A)
    return h.reshape(bt, t, MIX_WIDTH) * jax.nn.sigmoid(o), C, n, m


def _ssd_chunked(xs, dt, a, bm, cm, h0):
    t = xs.shape[1]
    L = min(CHUNK_B, t)
    xs, dt, bm, cm = [_chunks(_pad_time(z, L), L) for z in (xs, dt, bm, cm)]
    causal = jnp.tril(jnp.ones((L, L), bool))[None, :, :, None, None]

    def body(h, inp):
        xc, dtc, bc, cc = inp
        acs = jnp.cumsum(dtc * a, axis=1)
        seg = jnp.where(causal, acs[:, :, None] - acs[:, None, :], -jnp.inf)
        cb = jnp.einsum('btgn,bsgn->btsg', cc, bc)
        wts = cb[..., None] * jnp.exp(seg) * dtc[:, None]
        y = jnp.einsum('btsgj,bsgjp->btgjp', wts, xc)
        y = y + jnp.einsum('btgn,bgjpn->btgjp', cc, h) * jnp.exp(acs)[..., None]
        a_last = acs[:, -1]
        tail = jnp.exp(a_last[:, None] - acs) * dtc
        h = jnp.exp(a_last)[..., None, None] * h + jnp.einsum('bsgjp,bsgn->bgjpn', tail[..., None] * xc, bc)
        return h, y

    h, y = lax.scan(body, h0, (xs, dt, bm, cm))
    return _unchunk(y, t), h


def _mamba_branch(ub, conv0, h0, conv_w, conv_b, dt_bias, a_log, d_skip, norm_g):
    bt, t, _ = ub.shape
    z, xbc, dt = _split(ub, [MIX_WIDTH, CONV_DIM, H_B])
    ext = jnp.concatenate([conv0, xbc], axis=1)
    conv_new = ext[:, t:]
    y = conv_b
    for w in range(CONV_W):
        y = y + ext[:, w:w + t] * conv_w[w]
    y = jax.nn.silu(y)
    xs, bm, cm = _split(y, [MIX_WIDTH, G_B * N_B, G_B * N_B])
    dt = jax.nn.softplus(dt + dt_bias).reshape(bt, t, G_B, J_B)
    a = -jnp.exp(a_log).reshape(G_B, J_B)
    xs = xs.reshape(bt, t, G_B, J_B, P_B)
    ys, h = _ssd_chunked(xs, dt, a, bm.reshape(bt, t, G_B, N_B), cm.reshape(bt, t, G_B, N_B),
                         h0.reshape(bt, G_B, J_B, P_B, N_B))
    ys = ys + d_skip.reshape(G_B, J_B, 1) * xs
    ys = (ys.reshape(bt, t, MIX_WIDTH) * jax.nn.silu(z)).reshape(bt, t, G_B, MIX_WIDTH // G_B)
    ys = ys * lax.rsqrt(jnp.mean(ys * ys, -1, keepdims=True) + EPS)
    return ys.reshape(bt, t, MIX_WIDTH) * norm_g, conv_new, h.reshape(bt, H_B, P_B, N_B)


def _rwkv_scan(r, decay, k, v, kk, a, s0):
    def step(s, inp):
        r_t, w_t, k_t, v_t, kk_t, a_t = inp
        sa = jnp.einsum('bhij,bhj->bhi', s, kk_t)
        s = s * w_t[:, :, None, :] - sa[..., None] * (kk_t * a_t)[:, :, None, :] + v_t[..., None] * k_t[:, :, None, :]
        return s, jnp.einsum('bhij,bhj->bhi', s, r_t)

    seq = tuple(jnp.moveaxis(z, 1, 0) for z in (r, decay, k, v, kk, a))
    s, y = lax.scan(step, s0, seq)
    return jnp.moveaxis(y, 0, 1), s


def _rwkv_branch(uc, shift0, s0, mu, w0, w2, a0, a2, g2, k_k, k_a, r_k, ln_w, ln_b):
    bt, t, _ = uc.shape
    prev = jnp.concatenate([shift0[:, None], uc[:, :-1]], axis=1)
    xm = uc + (prev - uc) * mu
    r, k, v, wl, al, gl = _split(xm, [MIX_WIDTH, MIX_WIDTH, MIX_WIDTH, LORA_W, LORA_A, LORA_G])
    w = -jax.nn.softplus(-(w0 + jnp.tanh(wl) @ w2)) - 0.5
    decay = jnp.exp(-jnp.exp(w))
    a = jax.nn.sigmoid(a0 + al @ a2)
    g = jax.nn.sigmoid(gl) @ g2

    def heads(zz):
        return zz.reshape(bt, t, H_C, K_C)

    kk = heads(k * k_k)
    kk = kk / jnp.maximum(jnp.sqrt(jnp.sum(kk * kk, -1, keepdims=True)), 1e-12)
    k = k * (1.0 + (a - 1.0) * k_a)
    r, k, v, decay, a = heads(r), heads(k), heads(v), heads(decay), heads(a)
    y, s = _rwkv_scan(r, decay, k, v, kk, a, s0)
    mean = jnp.mean(y, -1, keepdims=True)
    var = jnp.mean(jnp.square(y - mean), -1, keepdims=True)
    y = ((y - mean) * lax.rsqrt(var + LN_EPS_C)).reshape(bt, t, MIX_WIDTH) * ln_w + ln_b
    bonus = jnp.sum(r * k * r_k, -1, keepdims=True) * v
    y = y + bonus.reshape(bt, t, MIX_WIDTH)
    return y * g, uc[:, -1], s


def _swiglu(x, wg, wu, wd):
    return (jax.nn.silu(x @ wg) * (x @ wu)) @ wd


def _moe(x, router, wg, wu, wd):
    logits = (x @ router).astype(jnp.float32)
    top_v, top_i = lax.top_k(logits, TOP_K)
    top_w = jax.nn.softmax(top_v, axis=-1)
    combine = jnp.sum(jax.nn.one_hot(top_i, N_EXP, dtype=jnp.float32) * top_w[..., None], axis=-2)
    out = jnp.zeros(x.shape, jnp.float32)
    for e in range(N_EXP):
        out = out + combine[..., e:e + 1] * _swiglu(x, wg[e], wu[e], wd[e])
    return out.astype(x.dtype)


def _trunk(x, mC, mn, mm, ssm, conv, wkv, shift, p):
    f32 = jnp.float32
    new = [[] for _ in range(7)]
    for l in range(DEPTH):
        xn = _rmsnorm(x, p['norm_mix'][l])
        u = (xn @ p['w_in'][l]).astype(f32)
        ua, ub, uc, ug = _split(u, [A_COLS, B_COLS, R_W, GATE_COLS])
        ya, C_, n_, m_ = _mlstm_branch(ua, mC[l].astype(f32), mn[l].astype(f32), mm[l].astype(f32),
                                       p['mlstm_i_bias'][l], p['mlstm_f_bias'][l], p['mlstm_norm'][l])
        yb, conv_, h_ = _mamba_branch(ub, conv[l].astype(f32), ssm[l].astype(f32), p['mamba_conv_w'][l],
                                      p['mamba_conv_b'][l], p['mamba_dt_bias'][l], p['mamba_A_log'][l],
                                      p['mamba_D'][l], p['mamba_norm'][l])
        yc, shift_, s_ = _rwkv_branch(uc, shift[l].astype(f32), wkv[l].astype(f32), p['rwkv_mu'][l],
                                      p['rwkv_w0'][l], p['rwkv_w2'][l], p['rwkv_a0'][l], p['rwkv_a2'][l],
                                      p['rwkv_g2'][l], p['rwkv_k_k'][l], p['rwkv_k_a'][l], p['rwkv_r_k'][l],
                                      p['rwkv_ln_w'][l], p['rwkv_ln_b'][l])
        ga, gb, gc = _split(jax.nn.sigmoid(ug), [D_MODEL, D_MODEL, D_MODEL])
        mixed = (ga * (ya.astype(x.dtype) @ p['w_branch_a'][l])
                 + gb * (yb.astype(x.dtype) @ p['w_branch_b'][l])
                 + gc * (yc.astype(x.dtype) @ p['w_branch_c'][l]))
        x = x + (mixed.astype(x.dtype) @ p['w_out'][l]).astype(x.dtype)
        xn = _rmsnorm(x, p['norm_ffn'][l])
        if l % 2 == 0:
            j = l // 2
            f = _swiglu(xn, p['ffn_w_gate'][j], p['ffn_w_up'][j], p['ffn_w_down'][j])
        else:
            j = l // 2
            f = _moe(xn, p['moe_router'][j], p['moe_w_gate'][j], p['moe_w_up'][j], p['moe_w_down'][j])
        x = x + f.astype(x.dtype)
        for lst, val in zip(new, (C_, n_, m_, h_, conv_, s_, shift_)):
            lst.append(val)
    y = _rmsnorm(x, p['norm_final'])
    refs = (mC, mn, mm, ssm, conv, wkv, shift)
    st = [jnp.stack(lst).astype(ref.dtype) for lst, ref in zip(new, refs)]
    return y, st[0], st[1], st[2], st[3], st[4], st[5], st[6]


def setup_inputs(seed: int = 0) -> dict:
    key = jax.random.key(seed)
    keys = iter(jax.random.split(key, 64))

    def nrm(shape, scale=1.0):
        return jax.random.normal(next(keys), shape, jnp.float32) * scale

    def unif(shape, lo, hi):
        return jax.random.uniform(next(keys), shape, jnp.float32, lo, hi)

    L = DEPTH
    dt_init = jnp.exp(unif((L, H_B), np.log(1e-3), np.log(1e-1)))
    return {
        "x_prompt": nrm((BATCH, SEQ, D_MODEL)),
        "x_sample": nrm((DEC_BATCH, DEC_SEQ, D_MODEL)),
        "state_mlstm_C": nrm((L, DEC_BATCH, H_A, DK_A, DV_A), 0.1),
        "state_mlstm_n": nrm((L, DEC_BATCH, H_A, DK_A), 0.1),
        "state_mlstm_m": nrm((L, DEC_BATCH, H_A)),
        "state_ssm": nrm((L, DEC_BATCH, H_B, P_B, N_B), 0.1),
        "state_conv": nrm((L, DEC_BATCH, CONV_W - 1, CONV_DIM)),
        "state_wkv": nrm((L, DEC_BATCH, H_C, K_C, K_C), 0.1),
        "state_shift": nrm((L, DEC_BATCH, R_W)),
        "norm_mix": 1.0 + nrm((L, D_MODEL), 0.02),
        "w_in": nrm((L, D_MODEL, P_TOT), D_MODEL ** -0.5),
        "mlstm_i_bias": nrm((L, H_A), 0.1) - 1.0,
        "mlstm_f_bias": 3.0 + nrm((L, H_A), 0.1),
        "mlstm_norm": 1.0 + nrm((L, MIX_WIDTH), 0.02),
        "mamba_conv_w": nrm((L, CONV_W, CONV_DIM), CONV_W ** -0.5),
        "mamba_conv_b": nrm((L, CONV_DIM), 0.02),
        "mamba_dt_bias": jnp.log(jnp.expm1(dt_init)),
        "mamba_A_log": jnp.log(unif((L, H_B), 1.0, 16.0)),
        "mamba_D": 1.0 + nrm((L, H_B), 0.02),
        "mamba_norm": 1.0 + nrm((L, MIX_WIDTH), 0.02),
        "rwkv_mu": unif((L, R_W), 0.0, 1.0),
        "rwkv_w0": unif((L, MIX_WIDTH), -5.0, 0.0),
        "rwkv_w2": nrm((L, LORA_W, MIX_WIDTH), 0.1 * LORA_W ** -0.5),
        "rwkv_a0": nrm((L, MIX_WIDTH), 0.1),
        "rwkv_a2": nrm((L, LORA_A, MIX_WIDTH), 0.1 * LORA_A ** -0.5),
        "rwkv_g2": nrm((L, LORA_G, MIX_WIDTH), LORA_G ** -0.5),
        "rwkv_k_k": 0.85 + nrm((L, MIX_WIDTH), 0.02),
        "rwkv_k_a": 1.0 + nrm((L, MIX_WIDTH), 0.02),
        "rwkv_r_k": nrm((L, H_C, K_C), 0.1),
        "rwkv_ln_w": 1.0 + nrm((L, MIX_WIDTH), 0.02),
        "rwkv_ln_b": nrm((L, MIX_WIDTH), 0.02),
        "w_branch_a": nrm((L, MIX_WIDTH, D_MODEL), MIX_WIDTH ** -0.5),
        "w_branch_b": nrm((L, MIX_WIDTH, D_MODEL), MIX_WIDTH ** -0.5),
        "w_branch_c": nrm((L, MIX_WIDTH, D_MODEL), MIX_WIDTH ** -0.5),
        "w_out": nrm((L, D_MODEL, D_MODEL), D_MODEL ** -0.5),
        "norm_ffn": 1.0 + nrm((L, D_MODEL), 0.02),
        "ffn_w_gate": nrm((N_DENSE, D_MODEL, D_FF), D_MODEL ** -0.5),
        "ffn_w_up": nrm((N_DENSE, D_MODEL, D_FF), D_MODEL ** -0.5),
        "ffn_w_down": nrm((N_DENSE, D_FF, D_MODEL), D_FF ** -0.5),
        "moe_router": nrm((N_MOE, D_MODEL, N_EXP), D_MODEL ** -0.5),
        "moe_w_gate": nrm((N_MOE, N_EXP, D_MODEL, D_FF_E), D_MODEL ** -0.5),
        "moe_w_up": nrm((N_MOE, N_EXP, D_MODEL, D_FF_E), D_MODEL ** -0.5),
        "moe_w_down": nrm((N_MOE, N_EXP, D_FF_E, D_MODEL), D_FF_E ** -0.5),
        "norm_final": 1.0 + nrm((D_MODEL,), 0.02),
    }


def reference(x_prompt, x_sample, state_mlstm_C, state_mlstm_n, state_mlstm_m, state_ssm, state_conv,
              state_wkv, state_shift, norm_mix, w_in, mlstm_i_bias, mlstm_f_bias, mlstm_norm,
              mamba_conv_w, mamba_conv_b, mamba_dt_bias, mamba_A_log, mamba_D, mamba_norm,
              rwkv_mu, rwkv_w0, rwkv_w2, rwkv_a0, rwkv_a2, rwkv_g2, rwkv_k_k, rwkv_k_a, rwkv_r_k,
              rwkv_ln_w, rwkv_ln_b, w_branch_a, w_branch_b, w_branch_c, w_out, norm_ffn,
              ffn_w_gate, ffn_w_up, ffn_w_down, moe_router, moe_w_gate, moe_w_up, moe_w_down, norm_final):
    p = dict(norm_mix=norm_mix, w_in=w_in, mlstm_i_bias=mlstm_i_bias, mlstm_f_bias=mlstm_f_bias,
             mlstm_norm=mlstm_norm, mamba_conv_w=mamba_conv_w, mamba_conv_b=mamba_conv_b,
             mamba_dt_bias=mamba_dt_bias, mamba_A_log=mamba_A_log, mamba_D=mamba_D, mamba_norm=mamba_norm,
             rwkv_mu=rwkv_mu, rwkv_w0=rwkv_w0, rwkv_w2=rwkv_w2, rwkv_a0=rwkv_a0, rwkv_a2=rwkv_a2,
             rwkv_g2=rwkv_g2, rwkv_k_k=rwkv_k_k, rwkv_k_a=rwkv_k_a, rwkv_r_k=rwkv_r_k,
             rwkv_ln_w=rwkv_ln_w, rwkv_ln_b=rwkv_ln_b, w_branch_a=w_branch_a, w_branch_b=w_branch_b,
             w_branch_c=w_branch_c, w_out=w_out, norm_ffn=norm_ffn, ffn_w_gate=ffn_w_gate,
             ffn_w_up=ffn_w_up, ffn_w_down=ffn_w_down, moe_router=moe_router, moe_w_gate=moe_w_gate,
             moe_w_up=moe_w_up, moe_w_down=moe_w_down, norm_final=norm_final)
    nb = x_prompt.shape[0]

    def zeros_like_state(s):
        return jnp.zeros((s.shape[0], nb) + s.shape[2:], s.dtype)

    y_prompt, pC, pn, pm, pssm, pconv, pwkv, pshift = _trunk(
        x_prompt, zeros_like_state(state_mlstm_C), zeros_like_state(state_mlstm_n),
        zeros_like_state(state_mlstm_m), zeros_like_state(state_ssm), zeros_like_state(state_conv),
        zeros_like_state(state_wkv), zeros_like_state(state_shift), p)
    y_sample, sC, sn, sm, sssm, sconv, swkv, sshift = _trunk(
        x_sample, state_mlstm_C, state_mlstm_n, state_mlstm_m, state_ssm, state_conv, state_wkv,
        state_shift, p)
    return (y_prompt, y_sample, pC, pn, pm, pssm, pconv, pwkv, pshift, sC, sn, sm, sssm, sconv, swkv, sshift)
```

```python
import functools

import jax
import jax.numpy as jnp
from jax import lax
from jax.experimental import pallas as pl
from jax.experimental.pallas import tpu as pltpu

f32 = jnp.float32
bf16 = jnp.bfloat16

D_MODEL = 2048
MIX = D_MODEL // 2
H_A, DK_A, DV_A = 4, 128, 256
GATE_CAP = 15.0
H_B, P_B, N_B, G_B = 16, 64, 128, 2
CONV_W = 4
CONV_DIM = MIX + 2 * G_B * N_B
H_C, K_C = 16, 64
LORA_W, LORA_A, LORA_G = 96, 96, 256
R_W = 3 * MIX + LORA_W + LORA_A + LORA_G
A_COLS = 2 * H_A * DK_A + 2 * MIX + 2 * H_A
B_COLS = MIX + CONV_DIM + H_B
D_FF = 5632
N_EXP = 8
D_FF_E = 2816
EPS = 1e-6
LN_EPS_C = 64e-5
NEG_BIG = -1e30

C_Q, C_K, C_V, C_O, C_Z, C_UG = 0, 512, 1024, 2048, 3072, 4096
C_R, C_KC, C_VC, C_XS, C_BC, C_LORA, C_GATE = 10240, 11264, 12288, 13312, 14336, 14848, 15360
P_PAD = 15872
LANES = 128
GL_I, GL_F, GL_DT = 0, H_A, 2 * H_A

VMEM_LIMIT = 56 * 1024 * 1024


def _cp(sem):
    return pltpu.CompilerParams(dimension_semantics=sem, vmem_limit_bytes=VMEM_LIMIT)


def _bf(x):
    return x.astype(bf16)


def _dot(a, b):
    return jnp.dot(_bf(a), _bf(b), preferred_element_type=f32)


def _dot_nt(a, b):
    return lax.dot_general(_bf(a), _bf(b), (((1,), (1,)), ((), ())), preferred_element_type=f32)


def _dot_tn(a, b):
    return lax.dot_general(_bf(a), _bf(b), (((0,), (0,)), ((), ())), preferred_element_type=f32)


def _split3(x):
    hi = _bf(x)
    r1 = x - hi.astype(f32)
    mid = _bf(r1)
    lo = _bf(r1 - mid.astype(f32))
    return hi, mid, lo


def _dot3(a01, x):
    hi, mid, lo = _split3(x)
    d = lambda p: jnp.dot(a01, p, preferred_element_type=f32)
    return d(hi) + d(mid) + d(lo)


def _dot3_tn(x, b01):
    hi, mid, lo = _split3(x)
    d = lambda p: lax.dot_general(p, b01, (((0,), (0,)), ((), ())), preferred_element_type=f32)
    return d(hi) + d(mid) + d(lo)


def _dotx3(a, b):
    a_hi = _bf(a)
    a_lo = _bf(a - a_hi.astype(f32))
    b_hi = _bf(b)
    b_lo = _bf(b - b_hi.astype(f32))
    d = lambda p, q: jnp.dot(p, q, preferred_element_type=f32)
    return d(a_hi, b_hi) + d(a_hi, b_lo) + d(a_lo, b_hi)


def _sigmoid(x):
    return 1.0 / (1.0 + jnp.exp(-x))


def _silu(x):
    return x * _sigmoid(x)


def _softplus(x):
    return jnp.maximum(x, 0.0) + jnp.log1p(jnp.exp(-jnp.abs(x)))


def _iota2(shape, axis):
    return lax.broadcasted_iota(jnp.int32, shape, axis)


def _tri_masks(L):
    r = _iota2((L, L), 0)
    c = _iota2((L, L), 1)
    tril_b = c <= r
    tril = jnp.where(tril_b, 1.0, 0.0).astype(bf16)
    triu = jnp.where(r <= c, 1.0, 0.0).astype(bf16)
    eye = jnp.where(r == c, 1.0, 0.0).astype(bf16)
    return tril_b, tril, triu, eye


def _rmsnorm_body(x_ref, g_ref, o_ref):
    x = x_ref[...]
    y = x * lax.rsqrt(jnp.mean(x * x, axis=-1, keepdims=True) + EPS) * g_ref[...]
    o_ref[...] = y.astype(o_ref.dtype)


def _rmsnorm(x, g, out_dtype, tm=512):
    m, d = x.shape
    return pl.pallas_call(
        _rmsnorm_body,
        grid=(m // tm,),
        in_specs=[pl.BlockSpec((tm, d), lambda i: (i, 0)), pl.BlockSpec((1, d), lambda i: (0, 0))],
        out_specs=pl.BlockSpec((tm, d), lambda i: (i, 0)),
        out_shape=jax.ShapeDtypeStruct((m, d), out_dtype),
        compiler_params=_cp(("parallel",)),
        name="rmsnorm",
    )(x, g.reshape(1, d))


def _proj_in_body(x_ref, w_ref, o_ref):
    o_ref[...] = jnp.dot(x_ref[...], w_ref[...], preferred_element_type=f32)


def _proj_in(xn, w, tm=512, tn=512):
    m, k = xn.shape
    n = w.shape[1]
    return pl.pallas_call(
        _proj_in_body,
        grid=(n // tn, m // tm),
        in_specs=[pl.BlockSpec((tm, k), lambda j, i: (i, 0)), pl.BlockSpec((k, tn), lambda j, i: (0, j))],
        out_specs=pl.BlockSpec((tm, tn), lambda j, i: (i, j)),
        out_shape=jax.ShapeDtypeStruct((m, n), f32),
        compiler_params=_cp(("parallel", "parallel")),
        name="proj_in",
    )(xn, w)


def _branch_body(ya_ref, yb_ref, yc_ref, ga_ref, gb_ref, gc_ref, wa_ref, wb_ref, wc_ref, o_ref):
    d = lambda y, w: jnp.dot(y[...], w[...], preferred_element_type=f32)
    acc = (_sigmoid(ga_ref[...]) * d(ya_ref, wa_ref) + _sigmoid(gb_ref[...]) * d(yb_ref, wb_ref)
           + _sigmoid(gc_ref[...]) * d(yc_ref, wc_ref))
    o_ref[...] = acc.astype(o_ref.dtype)


def _branch_mix(ya, yb, yc, u, wa, wb, wc, tm=512, tn=512):
    m = ya.shape[0]
    gb0 = C_UG // tn
    nb = D_MODEL // tn
    yspec = pl.BlockSpec((tm, MIX), lambda j, i: (i, 0))
    wspec = pl.BlockSpec((MIX, tn), lambda j, i: (0, j))
    gspec = lambda off: pl.BlockSpec((tm, tn), lambda j, i: (i, gb0 + off * nb + j))
    return pl.pallas_call(
        _branch_body,
        grid=(nb, m // tm),
        in_specs=[yspec, yspec, yspec, gspec(0), gspec(1), gspec(2), wspec, wspec, wspec],
        out_specs=pl.BlockSpec((tm, tn), lambda j, i: (i, j)),
        out_shape=jax.ShapeDtypeStruct((m, D_MODEL), bf16),
        compiler_params=_cp(("parallel", "parallel")),
        name="branch_mix",
    )(ya, yb, yc, u, u, u, wa, wb, wc)


def _outproj_body(x_ref, mx_ref, w_ref, g_ref, x1_ref, xn_ref):
    x1 = x_ref[...] + jnp.dot(mx_ref[...], w_ref[...], preferred_element_type=f32)
    x1_ref[...] = x1
    xn = x1 * lax.rsqrt(jnp.mean(x1 * x1, axis=-1, keepdims=True) + EPS) * g_ref[...]
    xn_ref[...] = xn.astype(xn_ref.dtype)


def _outproj(x, mixed, w, g, tm=256):
    m, d = x.shape
    row = pl.BlockSpec((tm, d), lambda i: (i, 0))
    return pl.pallas_call(
        _outproj_body,
        grid=(m // tm,),
        in_specs=[row, row, pl.BlockSpec((d, d), lambda i: (0, 0)), pl.BlockSpec((1, d), lambda i: (0, 0))],
        out_specs=[row, row],
        out_shape=[jax.ShapeDtypeStruct((m, d), f32), jax.ShapeDtypeStruct((m, d), bf16)],
        compiler_params=_cp(("parallel",)),
        name="outproj",
    )(x, mixed, w, g.reshape(1, d))


def _ffn_up_body(x_ref, wg_ref, wu_ref, o_ref, wgb, wub):
    @pl.when(pl.program_id(1) == 0)
    def _():
        wgb[...] = _bf(wg_ref[...])
        wub[...] = _bf(wu_ref[...])

    x = x_ref[...]
    a = jnp.dot(x, wgb[...], preferred_element_type=f32)
    b = jnp.dot(x, wub[...], preferred_element_type=f32)
    o_ref[...] = (_silu(a) * b).astype(o_ref.dtype)


def _ffn_up(xn, wg, wu, tm=512, tn=512):
    m, k = xn.shape
    n = wg.shape[1]
    wspec = pl.BlockSpec((k, tn), lambda j, i: (0, j))
    return pl.pallas_call(
        _ffn_up_body,
        grid=(n // tn, m // tm),
        in_specs=[pl.BlockSpec((tm, k), lambda j, i: (i, 0)), wspec, wspec],
        out_specs=pl.BlockSpec((tm, tn), lambda j, i: (i, j)),
        out_shape=jax.ShapeDtypeStruct((m, n), bf16),
        scratch_shapes=[pltpu.VMEM((k, tn), bf16), pltpu.VMEM((k, tn), bf16)],
        compiler_params=_cp(("parallel", "arbitrary")),
        name="ffn_up",
    )(xn, wg, wu)


def _ffn_down_body(h_ref, w_ref, x_ref, o_ref, wb):
    @pl.when(pl.program_id(1) == 0)
    def _():
        wb[...] = _bf(w_ref[...])

    o_ref[...] = x_ref[...] + jnp.dot(h_ref[...], wb[...], preferred_element_type=f32)


def _ffn_down(h, w, x, tm=256, tn=512):
    m, k = h.shape
    n = w.shape[1]
    return pl.pallas_call(
        _ffn_down_body,
        grid=(n // tn, m // tm),
        in_specs=[pl.BlockSpec((tm, k), lambda j, i: (i, 0)), pl.BlockSpec((k, tn), lambda j, i: (0, j)),
                  pl.BlockSpec((tm, tn), lambda j, i: (i, j))],
        out_specs=pl.BlockSpec((tm, tn), lambda j, i: (i, j)),
        out_shape=jax.ShapeDtypeStruct((m, n), f32),
        scratch_shapes=[pltpu.VMEM((k, tn), bf16)],
        compiler_params=_cp(("parallel", "arbitrary")),
        name="ffn_down",
    )(h, w, x)


def _router_body(x_ref, g_ref, r_ref, ti_ref, tw_ref):
    x = x_ref[...]
    xn = x * lax.rsqrt(jnp.mean(x * x, axis=-1, keepdims=True) + EPS) * g_ref[...]
    logits = jnp.dot(xn, r_ref[...], preferred_element_type=f32, precision=lax.Precision.HIGHEST)
    lane_i = _iota2(logits.shape, 1)
    lane = lane_i.astype(f32)
    lg = jnp.where(lane_i < N_EXP, logits, -jnp.inf)
    v1 = jnp.max(lg, axis=1, keepdims=True)
    i1 = jnp.min(jnp.where(lg == v1, lane, float(LANES)), axis=1, keepdims=True)
    lg2 = jnp.where(lane == i1, -jnp.inf, lg)
    v2 = jnp.max(lg2, axis=1, keepdims=True)
    i2 = jnp.min(jnp.where(lg2 == v2, lane, float(LANES)), axis=1, keepdims=True)
    e2 = jnp.exp(v2 - v1)
    w1 = 1.0 / (1.0 + e2)
    w2 = e2 / (1.0 + e2)
    ti_ref[...] = jnp.where(lane_i == 0, i1, jnp.where(lane_i == 1, i2, 0.0)).astype(jnp.int32)
    tw_ref[...] = jnp.where(lane_i == 0, w1, jnp.where(lane_i == 1, w2, 0.0))


def _router(x, g, router, tm=512):
    m, d = x.shape
    rp = jnp.zeros((d, LANES), f32).at[:, :N_EXP].set(router)
    row = pl.BlockSpec((tm, LANES), lambda i: (i, 0))
    return pl.pallas_call(
        _router_body,
        grid=(m // tm,),
        in_specs=[pl.BlockSpec((tm, d), lambda i: (i, 0)), pl.BlockSpec((1, d), lambda i: (0, 0)),
                  pl.BlockSpec((d, LANES), lambda i: (0, 0))],
        out_specs=[row, row],
        out_shape=[jax.ShapeDtypeStruct((m, LANES), jnp.int32), jax.ShapeDtypeStruct((m, LANES), f32)],
        compiler_params=_cp(("parallel",)),
        name="router",
    )(x, g.reshape(1, d), rp)


def _new_expert(te_ref, i):
    prev = te_ref[jnp.maximum(i - 1, 0)]
    return jnp.logical_or(i == 0, te_ref[i] != prev)


def _moe_up_body(te_ref, tv_ref, x_ref, wg_ref, wu_ref, o_ref, wgb, wub):
    i = pl.program_id(1)

    @pl.when(_new_expert(te_ref, i))
    def _():
        wgb[...] = _bf(wg_ref[0])
        wub[...] = _bf(wu_ref[0])

    @pl.when(tv_ref[i] > 0)
    def _():
        x = x_ref[...]
        a = jnp.dot(x, wgb[...], preferred_element_type=f32)
        b = jnp.dot(x, wub[...], preferred_element_type=f32)
        o_ref[...] = (_silu(a) * b).astype(o_ref.dtype)

    @pl.when(tv_ref[i] == 0)
    def _():
        o_ref[...] = jnp.zeros(o_ref.shape, o_ref.dtype)


def _moe_up(xg, wg, wu, te, tv, tm, tn=256):
    p, k = xg.shape
    n = wg.shape[2]
    wspec = pl.BlockSpec((1, k, tn), lambda j, i, te, tv: (te[i], 0, j))
    return pl.pallas_call(
        _moe_up_body,
        grid_spec=pltpu.PrefetchScalarGridSpec(
            num_scalar_prefetch=2,
            grid=(n // tn, p // tm),
            in_specs=[pl.BlockSpec((tm, k), lambda j, i, te, tv: (i, 0)), wspec, wspec],
            out_specs=pl.BlockSpec((tm, tn), lambda j, i, te, tv: (i, j)),
            scratch_shapes=[pltpu.VMEM((k, tn), bf16), pltpu.VMEM((k, tn), bf16)],
        ),
        out_shape=jax.ShapeDtypeStruct((p, n), bf16),
        compiler_params=_cp(("parallel", "arbitrary")),
        name="moe_up",
    )(te, tv, xg, wg, wu)


def _moe_down_body(te_ref, tv_ref, h_ref, w_ref, rw_ref, o_ref, wb):
    i = pl.program_id(1)

    @pl.when(_new_expert(te_ref, i))
    def _():
        wb[...] = _bf(w_ref[0])

    @pl.when(tv_ref[i] > 0)
    def _():
        o_ref[...] = jnp.dot(h_ref[...], wb[...], preferred_element_type=f32) * rw_ref[:, 0:1]

    @pl.when(tv_ref[i] == 0)
    def _():
        o_ref[...] = jnp.zeros(o_ref.shape, o_ref.dtype)


def _moe_down(hg, wd, roww, te, tv, tm, tn=512):
    p, k = hg.shape
    n = wd.shape[2]
    return pl.pallas_call(
        _moe_down_body,
        grid_spec=pltpu.PrefetchScalarGridSpec(
            num_scalar_prefetch=2,
            grid=(n // tn, p // tm),
            in_specs=[pl.BlockSpec((tm, k), lambda j, i, te, tv: (i, 0)),
                      pl.BlockSpec((1, k, tn), lambda j, i, te, tv: (te[i], 0, j)),
                      pl.BlockSpec((tm, LANES), lambda j, i, te, tv: (i, 0))],
            out_specs=pl.BlockSpec((tm, tn), lambda j, i, te, tv: (i, j)),
            scratch_shapes=[pltpu.VMEM((k, tn), bf16)],
        ),
        out_shape=jax.ShapeDtypeStruct((p, n), f32),
        compiler_params=_cp(("parallel", "arbitrary")),
        name="moe_down",
    )(te, tv, hg, wd, roww)


def _moe(x1, xn_bf, g, router, wg, wu, wd, tm=256):
    m = x1.shape[0]
    top_i, top_w = _router(x1, g, router)
    e_flat = top_i[:, :2].reshape(-1)
    w_flat = top_w[:, :2].reshape(-1)
    npair = 2 * m
    ntile = npair // tm + N_EXP
    ptot = ntile * tm
    order = jnp.argsort(e_flat, stable=True)
    counts = jnp.sum(jax.nn.one_hot(e_flat, N_EXP, dtype=jnp.int32), axis=0)
    tiles_per = (counts + tm - 1) // tm
    tile_end = jnp.cumsum(tiles_per)
    gstart = (tile_end - tiles_per) * tm
    cstart = jnp.cumsum(counts) - counts
    e_sorted = e_flat[order]
    rank = jnp.arange(npair, dtype=jnp.int32) - cstart[e_sorted]
    dest = gstart[e_sorted] + rank
    src_tok = jnp.zeros((ptot,), jnp.int32).at[dest].set((order // 2).astype(jnp.int32))
    roww = jnp.zeros((ptot,), f32).at[dest].set(w_flat[order])
    pos = jnp.zeros((npair,), jnp.int32).at[order].set(dest.astype(jnp.int32))
    tidx = jnp.arange(ntile, dtype=jnp.int32)
    te = jnp.minimum(jnp.sum((tidx[:, None] >= tile_end[None, :]).astype(jnp.int32), axis=1), N_EXP - 1)
    tv = (tidx < tile_end[-1]).astype(jnp.int32)
    te = jnp.where(tv > 0, te, te[jnp.maximum(tile_end[-1] - 1, 0)]).astype(jnp.int32)

    xg = jnp.take(xn_bf, src_tok, axis=0)
    hg = _moe_up(xg, wg, wu, te, tv, tm)
    yg = _moe_down(hg, wd, jnp.broadcast_to(roww[:, None], (ptot, LANES)), te, tv, tm)
    pos2 = pos.reshape(m, 2)
    return x1 + jnp.take(yg, pos2[:, 0], axis=0) + jnp.take(yg, pos2[:, 1], axis=0)


def _mlstm_body(q_ref, k_ref, v_ref, o_ref, g_ref, bias_ref, ng_ref, c0_ref, n0_ref, m0_ref,
                y_ref, c_ref, n_ref, m_ref, *, L, t_valid):
    c = pl.program_id(1)

    @pl.when(c == 0)
    def _():
        c_ref[...] = c0_ref[...]
        n_ref[...] = n0_ref[...]
        m_ref[...] = m0_ref[...]

    tril_b, tril, triu, eye = _tri_masks(L)
    tg = GATE_CAP * jnp.tanh((g_ref[...] + bias_ref[...]) / GATE_CAP)
    li_all = tg
    lf_all = jnp.minimum(tg, 0.0) - jnp.log1p(jnp.exp(-jnp.abs(tg)))
    if t_valid is not None:
        valid = (c * L + _iota2((L, LANES), 0)) < t_valid
        li_all = jnp.where(valid, li_all, NEG_BIG)
        lf_all = jnp.where(valid, lf_all, 0.0)
    b_all = _dot3(tril, lf_all)
    b_rows = _dot3_tn(lf_all, triu)
    li_rows = _dot3_tn(li_all, eye)

    for h in range(H_A):
        q = q_ref[:, h * DK_A:(h + 1) * DK_A]
        k = k_ref[:, h * DK_A:(h + 1) * DK_A] * (DK_A ** -0.5)
        v = v_ref[:, h * DV_A:(h + 1) * DV_A]
        bc = b_all[:, GL_F + h:GL_F + h + 1]
        br = b_rows[GL_F + h:GL_F + h + 1, :]
        lr = li_rows[GL_I + h:GL_I + h + 1, :]
        lc = li_all[:, GL_I + h:GL_I + h + 1]
        m_prev = m_ref[0, h:h + 1, 0:1]
        c_prev = c_ref[0, h]
        n_prev = n_ref[0, h:h + 1, :]

        dmat = jnp.where(tril_b, bc - br + lr, -jnp.inf)
        g_inter = bc + m_prev
        m_t = jnp.maximum(g_inter, jnp.max(dmat, axis=1, keepdims=True))
        w_inter = jnp.exp(g_inter - m_t)
        s = _dot_nt(q, k) * jnp.exp(dmat - m_t)
        num = w_inter * _dot(q, c_prev) + _dot(s, v)
        den = w_inter * jnp.sum(q * n_prev, axis=1, keepdims=True) + jnp.sum(s, axis=1, keepdims=True)
        hh = num / jnp.maximum(jnp.abs(den), jnp.exp(-m_t))

        b_last = bc[L - 1:L, :]
        g_s = b_last - bc + lc
        m_new = jnp.maximum(b_last + m_prev, jnp.max(g_s, axis=0, keepdims=True))
        a_s = jnp.exp(g_s - m_new)
        decay = jnp.exp(b_last + m_prev - m_new)
        ak = a_s * k
        c_ref[0, h] = decay * c_prev + _dot_tn(ak, v)
        n_ref[0, h:h + 1, :] = decay * n_prev + jnp.sum(ak, axis=0, keepdims=True)
        m_ref[0, h:h + 1, :] = jnp.broadcast_to(m_new, (1, LANES))

        hn = hh * lax.rsqrt(jnp.mean(hh * hh, axis=-1, keepdims=True) + EPS) * ng_ref[:, h * DV_A:(h + 1) * DV_A]
        y = hn * _sigmoid(o_ref[:, h * DV_A:(h + 1) * DV_A])
        y_ref[:, h * DV_A:(h + 1) * DV_A] = y.astype(y_ref.dtype)


def _mlstm(u, bias_row, norm_g, c0, n0, m0, *, bt, t, L, t_valid):
    nc = t // L
    rb = lambda b, c: b * nc + c
    m0b = jnp.broadcast_to(m0[:, :, None], (bt, H_A, LANES))
    in_specs = [
        pl.BlockSpec((L, 512), lambda b, c: (rb(b, c), C_Q // 512)),
        pl.BlockSpec((L, 512), lambda b, c: (rb(b, c), C_K // 512)),
        pl.BlockSpec((L, MIX), lambda b, c: (rb(b, c), C_V // MIX)),
        pl.BlockSpec((L, MIX), lambda b, c: (rb(b, c), C_O // MIX)),
        pl.BlockSpec((L, LANES), lambda b, c: (rb(b, c), C_GATE // LANES)),
        pl.BlockSpec((1, LANES), lambda b, c: (0, 0)),
        pl.BlockSpec((1, MIX), lambda b, c: (0, 0)),
        pl.BlockSpec((1, H_A, DK_A, DV_A), lambda b, c: (b, 0, 0, 0)),
        pl.BlockSpec((1, H_A, DK_A), lambda b, c: (b, 0, 0)),
        pl.BlockSpec((1, H_A, LANES), lambda b, c: (b, 0, 0)),
    ]
    out_specs = [
        pl.BlockSpec((L, MIX), lambda b, c: (rb(b, c), 0)),
        pl.BlockSpec((1, H_A, DK_A, DV_A), lambda b, c: (b, 0, 0, 0)),
        pl.BlockSpec((1, H_A, DK_A), lambda b, c: (b, 0, 0)),
        pl.BlockSpec((1, H_A, LANES), lambda b, c: (b, 0, 0)),
    ]
    out_shape = [
        jax.ShapeDtypeStruct((bt * t, MIX), bf16 if L % 16 == 0 else f32),
        jax.ShapeDtypeStruct((bt, H_A, DK_A, DV_A), f32),
        jax.ShapeDtypeStruct((bt, H_A, DK_A), f32),
        jax.ShapeDtypeStruct((bt, H_A, LANES), f32),
    ]
    y, c_new, n_new, m_new = pl.pallas_call(
        functools.partial(_mlstm_body, L=L, t_valid=t_valid),
        grid=(bt, nc), in_specs=in_specs, out_specs=out_specs, out_shape=out_shape,
        compiler_params=_cp(("parallel", "arbitrary")), name="mlstm",
    )(u, u, u, u, u, bias_row, norm_g.reshape(1, MIX), c0, n0, m0b)
    return y, c_new, n_new, m_new[:, :, 0]


def _ssd_body(z_ref, xs_ref, bcr_ref, g_ref, bias_ref, cwx_ref, cwb_ref, cbx_ref, cbb_ref, alog_ref,
              d_ref, ng_ref, cx0_ref, cb0_ref, h0_ref, y_ref, h_ref, px_sc, pb_sc, *, L, t_valid):
    c = pl.program_id(1)

    @pl.when(c == 0)
    def _():
        h_ref[...] = h0_ref[...]
        px_sc[...] = jnp.zeros(px_sc.shape, f32)
        pb_sc[...] = jnp.zeros(pb_sc.shape, f32)
        px_sc[L - 8:L, :] = cx0_ref[0]
        pb_sc[L - 8:L, :] = cb0_ref[0]

    def conv(cur, prev, cw_ref, cb_ref):
        row = _iota2(cur.shape, 0)
        acc = cb_ref[...] + cw_ref[CONV_W - 1:CONV_W, :] * cur
        for sft in range(1, CONV_W):
            shifted = jnp.where(row >= sft, pltpu.roll(cur, sft, 0), pltpu.roll(prev, sft, 0))
            acc = acc + cw_ref[CONV_W - 1 - sft:CONV_W - sft, :] * shifted
        return _silu(acc)

    cur_x = xs_ref[...]
    cur_b = bcr_ref[...]
    xs = conv(cur_x, px_sc[...], cwx_ref, cbx_ref)
    bcm = conv(cur_b, pb_sc[...], cwb_ref, cbb_ref)
    px_sc[...] = cur_x
    pb_sc[...] = cur_b

    tril_b, tril, triu, eye = _tri_masks(L)
    dt_all = _softplus(g_ref[...] + bias_ref[...])
    if t_valid is not None:
        valid = (c * L + _iota2((L, LANES), 0)) < t_valid
        dt_all = jnp.where(valid, dt_all, 0.0)
    da_all = dt_all * (-jnp.exp(alog_ref[...]))
    acs_all = _dot3(tril, da_all)
    acs_rows = _dot3_tn(da_all, triu)
    dt_rows = _dot3_tn(dt_all, eye)

    lane0 = _iota2((L, LANES), 1) < P_B
    row0 = _iota2((2 * P_B, 1), 0) < P_B
    npair = H_B // 2
    ys = []
    for p in range(npair):
        grp = (2 * p) // (H_B // G_B)
        bm = bcm[:, grp * N_B:(grp + 1) * N_B]
        cm = bcm[:, G_B * N_B + grp * N_B:G_B * N_B + (grp + 1) * N_B]
        cb = _dot_nt(cm, bm)
        xp = xs[:, p * LANES:(p + 1) * LANES]
        wts, acs, tails, alast = [], [], [], []
        for hd in (2 * p, 2 * p + 1):
            ln = GL_DT + hd
            ac = acs_all[:, ln:ln + 1]
            ar = acs_rows[ln:ln + 1, :]
            seg = jnp.where(tril_b, ac - ar, -jnp.inf)
            wts.append(cb * jnp.exp(seg) * dt_rows[ln:ln + 1, :])
            al = ac[L - 1:L, :]
            acs.append(ac)
            alast.append(al)
            tails.append(jnp.exp(al - ac) * dt_all[:, ln:ln + 1])
        hp = h_ref[0, p]
        yp = jnp.where(lane0, _dot(wts[0], xp), _dot(wts[1], xp))
        yp = yp + _dot_nt(cm, hp) * jnp.where(lane0, jnp.exp(acs[0]), jnp.exp(acs[1]))
        tailx = xp * jnp.where(lane0, tails[0], tails[1])
        dec = jnp.where(row0, jnp.exp(alast[0]), jnp.exp(alast[1]))
        h_ref[0, p] = dec * hp + _dot_tn(tailx, bm)
        ys.append(yp + d_ref[:, p * LANES:(p + 1) * LANES] * xp)

    y = jnp.concatenate(ys, axis=1) * _silu(z_ref[...])
    gw = MIX // G_B
    outs = []
    for grp in range(G_B):
        sg = y[:, grp * gw:(grp + 1) * gw]
        outs.append(sg * lax.rsqrt(jnp.mean(sg * sg, axis=-1, keepdims=True) + EPS))
    y_ref[...] = (jnp.concatenate(outs, axis=1) * ng_ref[...]).astype(y_ref.dtype)


def _ssd(u, bias_row, conv_w, conv_b, a_log, d_skip, norm_g, conv0, h0, *, bt, t, L, t_valid):
    nc = t // L
    rb = lambda b, c: b * nc + c
    const = lambda shape: pl.BlockSpec(shape, lambda b, c: tuple(0 for _ in shape))
    alog_row = jnp.zeros((1, LANES), f32).at[0, GL_DT:GL_DT + H_B].set(a_log)
    d_row = jnp.repeat(d_skip, P_B).reshape(1, MIX)
    conv0p = jnp.pad(conv0, ((0, 0), (8 - (CONV_W - 1), 0), (0, 0)))
    npair = H_B // 2
    hp0 = h0.reshape(bt, npair, 2 * P_B, N_B)
    in_specs = [
        pl.BlockSpec((L, MIX), lambda b, c: (rb(b, c), C_Z // MIX)),
        pl.BlockSpec((L, MIX), lambda b, c: (rb(b, c), C_XS // MIX)),
        pl.BlockSpec((L, 512), lambda b, c: (rb(b, c), C_BC // 512)),
        pl.BlockSpec((L, LANES), lambda b, c: (rb(b, c), C_GATE // LANES)),
        const((1, LANES)), const((CONV_W, MIX)), const((CONV_W, 512)), const((1, MIX)), const((1, 512)),
        const((1, LANES)), const((1, MIX)), const((1, MIX)),
        pl.BlockSpec((1, 8, MIX), lambda b, c: (b, 0, 0)),
        pl.BlockSpec((1, 8, 512), lambda b, c: (b, 0, 0)),
        pl.BlockSpec((1, npair, 2 * P_B, N_B), lambda b, c: (b, 0, 0, 0)),
    ]
    out_specs = [
        pl.BlockSpec((L, MIX), lambda b, c: (rb(b, c), 0)),
        pl.BlockSpec((1, npair, 2 * P_B, N_B), lambda b, c: (b, 0, 0, 0)),
    ]
    out_shape = [jax.ShapeDtypeStruct((bt * t, MIX), bf16 if L % 16 == 0 else f32), jax.ShapeDtypeStruct((bt, npair, 2 * P_B, N_B), f32)]
    y, h_new = pl.pallas_call(
        functools.partial(_ssd_body, L=L, t_valid=t_valid),
        grid=(bt, nc), in_specs=in_specs, out_specs=out_specs, out_shape=out_shape,
        scratch_shapes=[pltpu.VMEM((L, MIX), f32), pltpu.VMEM((L, 512), f32)],
        compiler_params=_cp(("parallel", "arbitrary")), name="ssd",
    )(u, u, u, u, bias_row, conv_w[:, :MIX], conv_w[:, MIX:], conv_b[:MIX].reshape(1, MIX),
      conv_b[MIX:].reshape(1, 512), alog_row, d_row, norm_g.reshape(1, MIX),
      conv0p[:, :, :MIX], conv0p[:, :, MIX:], hp0)
    return y, h_new.reshape(bt, H_B, P_B, N_B)


PR_MU_R, PR_MU_K, PR_MU_V, PR_W0, PR_A0, PR_KK, PR_KA, PR_RK, PR_LNW, PR_LNB = range(10)


def _rwkv_body(r_ref, k_ref, v_ref, l_ref, par_ref, mul_ref, w2_ref, a2_ref, g2_ref,
               shr_ref, shk_ref, shv_ref, shl_ref, s0_ref, y_ref, s_ref,
               cr_sc, ck_sc, cv_sc, cl_sc, st_sc, *, L, t_valid):
    c = pl.program_id(1)
    nc = pl.num_programs(1)
    npair = H_C // 2
    L2 = 2 * L

    @pl.when(c == 0)
    def _():
        cr_sc[...] = shr_ref[0]
        ck_sc[...] = shk_ref[0]
        cv_sc[...] = shv_ref[0]
        cl_sc[...] = shl_ref[0]
        rowm = _iota2((2 * K_C, K_C), 0) < K_C
        for p in range(npair):
            x = s0_ref[0, 2 * p:2 * p + 2].reshape(2 * K_C, K_C)
            st_sc[p] = jnp.concatenate([jnp.where(rowm, x, 0.0), jnp.where(rowm, 0.0, x)], axis=1)

    par = lambda i: par_ref[i:i + 1, :]

    def shifted(cur_ref, carry, mu):
        cur = cur_ref[...]
        row = _iota2(cur.shape, 0)
        prev = jnp.where(row == 0, carry[0:1, :], pltpu.roll(cur, 1, 0))
        carry[0:1, :] = cur[L - 1:L, :]
        return cur + (prev - cur) * mu

    xr = shifted(r_ref, cr_sc, par(PR_MU_R))
    xk = shifted(k_ref, ck_sc, par(PR_MU_K))
    xv = shifted(v_ref, cv_sc, par(PR_MU_V))
    xl = shifted(l_ref, cl_sc, mul_ref[...])

    lw = jnp.dot(_bf(jnp.tanh(xl)), w2_ref[...], preferred_element_type=f32)
    la = jnp.dot(_bf(xl), a2_ref[...], preferred_element_type=f32)
    g = jnp.dot(_bf(_sigmoid(xl)), g2_ref[...], preferred_element_type=f32)
    wlog = -_softplus(-(par(PR_W0) + lw)) - 0.5
    logdec = -jnp.exp(wlog)
    a = _sigmoid(par(PR_A0) + la)
    kkr = xk * par(PR_KK)
    k2 = xk * (1.0 + (a - 1.0) * par(PR_KA))
    if t_valid is not None:
        valid = (c * L + _iota2((L, MIX), 0)) < t_valid
        logdec = jnp.where(valid, logdec, 0.0)
        kkr = jnp.where(valid, kkr, 0.0)
        k2 = jnp.where(valid, k2, 0.0)
        xv = jnp.where(valid, xv, 0.0)

    _, tril, _, _ = _tri_masks(L)
    cum = _dot3(tril, logdec)
    pfull = jnp.exp(cum)
    pinv = jnp.exp(-cum)
    pprev = jnp.exp(cum - logdec)
    p_last = pfull[L - 1:L, :]

    lane0 = _iota2((L, LANES), 1) < K_C
    r2 = _iota2((L2, L2), 0)
    c2 = _iota2((L2, L2), 1)
    lg2 = L.bit_length() - 1
    same = (r2 >> lg2) == (c2 >> lg2)
    strict = jnp.logical_and(same, (c2 & (L - 1)) < (r2 & (L - 1)))
    incl = jnp.logical_and(same, (c2 & (L - 1)) <= (r2 & (L - 1)))
    eye2 = jnp.where(r2 == c2, 1.0, 0.0)

    def hsum(x):
        s0 = jnp.sum(jnp.where(lane0, x, 0.0), axis=1, keepdims=True)
        s1 = jnp.sum(jnp.where(lane0, 0.0, x), axis=1, keepdims=True)
        return jnp.where(lane0, s0, s1)

    def stack(x):
        return jnp.concatenate([jnp.where(lane0, x, 0.0), jnp.where(lane0, 0.0, x)], axis=0)

    nsq = max(L.bit_length() - 2, 0)
    for p in range(npair):
        sl = slice(p * LANES, (p + 1) * LANES)
        kkp = kkr[:, sl]
        kap = kkp / jnp.maximum(jnp.sqrt(hsum(kkp * kkp)), 1e-12)
        beta = kap * a[:, sl]
        rp, kp, vp = xr[:, sl], k2[:, sl], xv[:, sl]
        kap2 = stack(kap * pprev[:, sl])
        bt2 = stack(beta * pinv[:, sl])
        kt2 = stack(kp * pinv[:, sl])
        rt2 = stack(rp * pfull[:, sl])
        v2 = stack(vp)
        n_kb = jnp.where(strict, _dot_nt(kap2, bt2), 0.0)
        a_kk = jnp.where(strict, _dot_nt(kap2, kt2), 0.0)
        a_rb = jnp.where(incl, _dot_nt(rt2, bt2), 0.0)
        a_rk = jnp.where(incl, _dot_nt(rt2, kt2), 0.0)
        tinv = eye2 - n_kb
        pw = n_kb
        for _ in range(nsq):
            pw = _dotx3(pw, pw)
            tinv = tinv + _dotx3(tinv, pw)
        s_prev = st_sc[p]
        u2 = _dotx3(tinv, _dot_nt(kap2, s_prev) + _dot(a_kk, v2))
        y2 = _dot_nt(rt2, s_prev) + _dot(a_rk, v2) - _dot(a_rb, u2)
        yp = y2[0:L, :] + y2[L:L2, :]
        pl_row = p_last[:, sl]
        st_sc[p] = s_prev * pl_row + _dot_tn(v2, kt2 * pl_row) - _dot_tn(u2, bt2 * pl_row)

        mean = hsum(yp) * (1.0 / K_C)
        dy = yp - mean
        var = hsum(dy * dy) * (1.0 / K_C)
        yn = dy * lax.rsqrt(var + LN_EPS_C) * par(PR_LNW)[:, sl] + par(PR_LNB)[:, sl]
        bonus = hsum(rp * kp * par(PR_RK)[:, sl]) * vp
        y_ref[:, sl] = ((yn + bonus) * g[:, sl]).astype(y_ref.dtype)

    @pl.when(c == nc - 1)
    def _():
        for p in range(npair):
            s = st_sc[p]
            s_ref[0, 2 * p] = s[0:K_C, 0:K_C]
            s_ref[0, 2 * p + 1] = s[K_C:2 * K_C, K_C:2 * K_C]


def _rwkv(u, par, mu_l, w2p, a2p, g2p, shift0, s0, *, bt, t, L, t_valid):
    nc = t // L
    rb = lambda b, c: b * nc + c
    const = lambda shape: pl.BlockSpec(shape, lambda b, c: tuple(0 for _ in shape))
    npair = H_C // 2
    pad8 = lambda x: jnp.pad(x[:, None, :], ((0, 0), (0, 7), (0, 0)))
    sh_r = pad8(shift0[:, 0:MIX])
    sh_k = pad8(shift0[:, MIX:2 * MIX])
    sh_v = pad8(shift0[:, 2 * MIX:3 * MIX])
    sh_l = pad8(jnp.pad(shift0[:, 3 * MIX:], ((0, 0), (0, 512 - (R_W - 3 * MIX)))))
    in_specs = [
        pl.BlockSpec((L, MIX), lambda b, c: (rb(b, c), C_R // MIX)),
        pl.BlockSpec((L, MIX), lambda b, c: (rb(b, c), C_KC // MIX)),
        pl.BlockSpec((L, MIX), lambda b, c: (rb(b, c), C_VC // MIX)),
        pl.BlockSpec((L, 512), lambda b, c: (rb(b, c), C_LORA // 512)),
        const((16, MIX)), const((1, 512)), const((512, MIX)), const((512, MIX)), const((512, MIX)),
        pl.BlockSpec((1, 8, MIX), lambda b, c: (b, 0, 0)),
        pl.BlockSpec((1, 8, MIX), lambda b, c: (b, 0, 0)),
        pl.BlockSpec((1, 8, MIX), lambda b, c: (b, 0, 0)),
        pl.BlockSpec((1, 8, 512), lambda b, c: (b, 0, 0)),
        pl.BlockSpec((1, H_C, K_C, K_C), lambda b, c: (b, 0, 0, 0)),
    ]
    out_specs = [
        pl.BlockSpec((L, MIX), lambda b, c: (rb(b, c), 0)),
        pl.BlockSpec((1, H_C, K_C, K_C), lambda b, c: (b, 0, 0, 0)),
    ]
    out_shape = [jax.ShapeDtypeStruct((bt * t, MIX), bf16 if L % 16 == 0 else f32), jax.ShapeDtypeStruct((bt, H_C, K_C, K_C), f32)]
    return pl.pallas_call(
        functools.partial(_rwkv_body, L=L, t_valid=t_valid),
        grid=(bt, nc), in_specs=in_specs, out_specs=out_specs, out_shape=out_shape,
        scratch_shapes=[pltpu.VMEM((8, MIX), f32), pltpu.VMEM((8, MIX), f32), pltpu.VMEM((8, MIX), f32),
                        pltpu.VMEM((8, 512), f32), pltpu.VMEM((npair, 2 * K_C, 2 * K_C), f32)],
        compiler_params=_cp(("parallel", "arbitrary")), name="rwkv",
    )(u, u, u, u, par, mu_l, w2p, a2p, g2p, sh_r, sh_k, sh_v, sh_l, s0)


def _permute_w_in(w):
    ob, oc, og = A_COLS, A_COLS + B_COLS, A_COLS + B_COLS + R_W
    z = lambda n: jnp.zeros((w.shape[0], n), w.dtype)
    segs = [
        w[:, 0:3072],
        w[:, ob:ob + MIX],
        w[:, og:og + 3 * D_MODEL],
        w[:, oc:oc + 3 * MIX],
        w[:, ob + MIX:ob + MIX + CONV_DIM],
        w[:, oc + 3 * MIX:oc + R_W], z(512 - (R_W - 3 * MIX)),
        w[:, 3072:3072 + 2 * H_A], w[:, ob + MIX + CONV_DIM:ob + B_COLS], z(LANES - 2 * H_A - H_B),
    ]
    out = jnp.concatenate(segs, axis=1)
    return jnp.pad(out, ((0, 0), (0, P_PAD - out.shape[1]))).astype(bf16)


def _pad_rows(w, before, total):
    return jnp.pad(w, ((before, total - before - w.shape[0]), (0, 0))).astype(bf16)


def kernel(x_prompt, x_sample, state_mlstm_C, state_mlstm_n, state_mlstm_m, state_ssm, state_conv, state_wkv, state_shift, norm_mix, w_in, mlstm_i_bias, mlstm_f_bias, mlstm_norm, mamba_conv_w, mamba_conv_b, mamba_dt_bias, mamba_A_log, mamba_D, mamba_norm, rwkv_mu, rwkv_w0, rwkv_w2, rwkv_a0, rwkv_a2, rwkv_g2, rwkv_k_k, rwkv_k_a, rwkv_r_k, rwkv_ln_w, rwkv_ln_b, w_branch_a, w_branch_b, w_branch_c, w_out, norm_ffn, ffn_w_gate, ffn_w_up, ffn_w_down, moe_router, moe_w_gate, moe_w_up, moe_w_down, norm_final):
    nb, seq, d = x_prompt.shape
    db, dseq, _ = x_sample.shape
    depth = w_in.shape[0]
    mp = nb * seq
    ms = db * dseq
    t_s = 8
    assert seq >= CONV_W - 1 and dseq >= CONV_W - 1 and dseq <= t_s

    x = jnp.concatenate([x_prompt.reshape(mp, d), x_sample.reshape(ms, d)], axis=0)
    xn = _rmsnorm(x, norm_mix[0], bf16)

    zeros_like_state = lambda s: jnp.zeros((nb,) + s.shape[2:], s.dtype)
    new_p = [[] for _ in range(7)]
    new_s = [[] for _ in range(7)]

    for l in range(depth):
        u = _proj_in(xn, _permute_w_in(w_in[l]))
        u_s = jnp.pad(u[mp:].reshape(db, dseq, P_PAD), ((0, 0), (0, t_s - dseq), (0, 0))).reshape(db * t_s, P_PAD)

        bias_row = jnp.zeros((1, LANES), f32)
        bias_row = bias_row.at[0, GL_I:GL_I + H_A].set(mlstm_i_bias[l]).at[0, GL_F:GL_F + H_A].set(mlstm_f_bias[l])
        bias_row = bias_row.at[0, GL_DT:GL_DT + H_B].set(mamba_dt_bias[l])

        par = jnp.zeros((16, MIX), f32)
        mu = rwkv_mu[l]
        for idx, val in ((PR_MU_R, mu[0:MIX]), (PR_MU_K, mu[MIX:2 * MIX]), (PR_MU_V, mu[2 * MIX:3 * MIX]),
                         (PR_W0, rwkv_w0[l]), (PR_A0, rwkv_a0[l]), (PR_KK, rwkv_k_k[l]), (PR_KA, rwkv_k_a[l]),
                         (PR_RK, rwkv_r_k[l].reshape(MIX)), (PR_LNW, rwkv_ln_w[l]), (PR_LNB, rwkv_ln_b[l])):
            par = par.at[idx].set(val)
        mu_l = jnp.pad(mu[3 * MIX:], (0, 512 - (R_W - 3 * MIX))).reshape(1, 512)
        w2p = _pad_rows(rwkv_w2[l], 0, 512)
        a2p = _pad_rows(rwkv_a2[l], LORA_W, 512)
        g2p = _pad_rows(rwkv_g2[l], LORA_W + LORA_A, 512)

        def mixers(uu, bt, t, t_valid, la, lb, lc, st):
            mC, mn, mm, ssm, conv, wkv, shift = st
            ya, c_, n_, m_ = _mlstm(uu, bias_row, mlstm_norm[l], mC, mn, mm, bt=bt, t=t, L=la, t_valid=t_valid)
            yb, h_ = _ssd(uu, bias_row, mamba_conv_w[l], mamba_conv_b[l], mamba_A_log[l], mamba_D[l],
                          mamba_norm[l], conv, ssm, bt=bt, t=t, L=lb, t_valid=t_valid)
            yc, s_ = _rwkv(uu, par, mu_l, w2p, a2p, g2p, shift, wkv, bt=bt, t=t, L=lc, t_valid=t_valid)
            return ya, yb, yc, (c_, n_, m_, h_, s_)

        st_p = tuple(zeros_like_state(s) for s in (state_mlstm_C, state_mlstm_n, state_mlstm_m, state_ssm,
                                                    state_conv, state_wkv, state_shift))
        st_s = (state_mlstm_C[l], state_mlstm_n[l], state_mlstm_m[l], state_ssm[l], state_conv[l],
                state_wkv[l], state_shift[l])
        ya_p, yb_p, yc_p, ns_p = mixers(u, nb, seq, None, 128, 128, 64, st_p)
        ya_s, yb_s, yc_s, ns_s = mixers(u_s, db, t_s, dseq, t_s, t_s, t_s, st_s)

        unpad = lambda y: y.reshape(db, t_s, MIX)[:, :dseq].reshape(ms, MIX).astype(bf16)
        ya = jnp.concatenate([ya_p, unpad(ya_s)], axis=0)
        yb = jnp.concatenate([yb_p, unpad(yb_s)], axis=0)
        yc = jnp.concatenate([yc_p, unpad(yc_s)], axis=0)

        up3 = u[:mp].reshape(nb, seq, P_PAD)
        us3 = u[mp:].reshape(db, dseq, P_PAD)
        conv_rows = lambda z, t: z[:, t - (CONV_W - 1):t, C_XS:C_XS + CONV_DIM]
        shift_row = lambda z, t: jnp.concatenate([z[:, t - 1, C_R:C_R + 3 * MIX],
                                                  z[:, t - 1, C_LORA:C_LORA + (R_W - 3 * MIX)]], axis=-1)
        for lst, ns, z3, t in ((new_p, ns_p, up3, seq), (new_s, ns_s, us3, dseq)):
            c_, n_, m_, h_, s_ = ns
            for slot, val in zip(lst, (c_, n_, m_, h_, conv_rows(z3, t), s_, shift_row(z3, t))):
                slot.append(val)

        mixed = _branch_mix(ya, yb, yc, u, w_branch_a[l].astype(bf16), w_branch_b[l].astype(bf16),
                            w_branch_c[l].astype(bf16))
        x1, xn2 = _outproj(x, mixed, w_out[l].astype(bf16), norm_ffn[l])
        j = l // 2
        if l % 2 == 0:
            h = _ffn_up(xn2, ffn_w_gate[j], ffn_w_up[j])
            x = _ffn_down(h, ffn_w_down[j], x1)
        else:
            x = _moe(x1, xn2, norm_ffn[l], moe_router[j], moe_w_gate[j], moe_w_up[j], moe_w_down[j])
        if l + 1 < depth:
            xn = _rmsnorm(x, norm_mix[l + 1], bf16)

    y = _rmsnorm(x, norm_final, f32)
    y_prompt = y[:mp].reshape(nb, seq, d)
    y_sample = y[mp:].reshape(db, dseq, d)
    refs = (state_mlstm_C, state_mlstm_n, state_mlstm_m, state_ssm, state_conv, state_wkv, state_shift)
    outs_p = [jnp.stack(lst).astype(r.dtype) for lst, r in zip(new_p, refs)]
    outs_s = [jnp.stack(lst).astype(r.dtype) for lst, r in zip(new_s, refs)]
    return (y_prompt, y_sample, *outs_p, *outs_s)
```

```python
import functools

import jax
import jax.numpy as jnp
from jax import lax
from jax.experimental import pallas as pl
from jax.experimental.pallas import tpu as pltpu

f32 = jnp.float32
bf16 = jnp.bfloat16

D_MODEL = 2048
MIX = D_MODEL // 2
H_A, DK_A, DV_A = 4, 128, 256
GATE_CAP = 15.0
H_B, P_B, N_B, G_B = 16, 64, 128, 2
CONV_W = 4
CONV_DIM = MIX + 2 * G_B * N_B
H_C, K_C = 16, 64
LORA_W, LORA_A, LORA_G = 96, 96, 256
R_W = 3 * MIX + LORA_W + LORA_A + LORA_G
A_COLS = 2 * H_A * DK_A + 2 * MIX + 2 * H_A
B_COLS = MIX + CONV_DIM + H_B
D_FF = 5632
N_EXP = 8
D_FF_E = 2816
EPS = 1e-6
LN_EPS_C = 64e-5
NEG_BIG = -1e30

C_Q, C_K, C_V, C_O, C_Z, C_UG = 0, 512, 1024, 2048, 3072, 4096
C_R, C_KC, C_VC, C_XS, C_BC, C_LORA, C_GATE = 10240, 11264, 12288, 13312, 14336, 14848, 15360
P_PAD = 16384
LANES = 128
GL_I, GL_F, GL_DT = 0, H_A, 2 * H_A

VMEM_LIMIT = 56 * 1024 * 1024


def _cp(sem):
    return pltpu.CompilerParams(dimension_semantics=sem, vmem_limit_bytes=VMEM_LIMIT)


def _bf(x):
    return x.astype(bf16)


def _dot(a, b):
    return jnp.dot(_bf(a), _bf(b), preferred_element_type=f32)


def _dot_nt(a, b):
    return lax.dot_general(_bf(a), _bf(b), (((1,), (1,)), ((), ())), preferred_element_type=f32)


def _dot_tn(a, b):
    return lax.dot_general(_bf(a), _bf(b), (((0,), (0,)), ((), ())), preferred_element_type=f32)


def _split3(x):
    hi = _bf(x)
    r1 = x - hi.astype(f32)
    mid = _bf(r1)
    lo = _bf(r1 - mid.astype(f32))
    return hi, mid, lo


def _dot3(a01, x):
    hi, mid, lo = _split3(x)
    d = lambda p: jnp.dot(a01, p, preferred_element_type=f32)
    return d(hi) + d(mid) + d(lo)


def _dot3_tn(x, b01):
    hi, mid, lo = _split3(x)
    d = lambda p: lax.dot_general(p, b01, (((0,), (0,)), ((), ())), preferred_element_type=f32)
    return d(hi) + d(mid) + d(lo)


def _dotx3(a, b):
    a_hi = _bf(a)
    a_lo = _bf(a - a_hi.astype(f32))
    b_hi = _bf(b)
    b_lo = _bf(b - b_hi.astype(f32))
    d = lambda p, q: jnp.dot(p, q, preferred_element_type=f32)
    return d(a_hi, b_hi) + d(a_hi, b_lo) + d(a_lo, b_hi)


def _sigmoid(x):
    return 1.0 / (1.0 + jnp.exp(-x))


def _silu(x):
    return x * _sigmoid(x)


def _softplus(x):
    return jnp.maximum(x, 0.0) + jnp.log1p(jnp.exp(-jnp.abs(x)))


def _iota2(shape, axis):
    return lax.broadcasted_iota(jnp.int32, shape, axis)


def _tri_masks(L):
    r = _iota2((L, L), 0)
    c = _iota2((L, L), 1)
    tril_b = c <= r
    tril = jnp.where(tril_b, 1.0, 0.0).astype(bf16)
    triu = jnp.where(r <= c, 1.0, 0.0).astype(bf16)
    eye = jnp.where(r == c, 1.0, 0.0).astype(bf16)
    return tril_b, tril, triu, eye


def _rmsnorm_body(x_ref, g_ref, o_ref):
    x = x_ref[...]
    y = x * lax.rsqrt(jnp.mean(x * x, axis=-1, keepdims=True) + EPS) * g_ref[...]
    o_ref[...] = y.astype(o_ref.dtype)


def _rmsnorm(x, g, out_dtype, tm=512):
    m, d = x.shape
    return pl.pallas_call(
        _rmsnorm_body,
        grid=(m // tm,),
        in_specs=[pl.BlockSpec((tm, d), lambda i: (i, 0)), pl.BlockSpec((1, d), lambda i: (0, 0))],
        out_specs=pl.BlockSpec((tm, d), lambda i: (i, 0)),
        out_shape=jax.ShapeDtypeStruct((m, d), out_dtype),
        compiler_params=_cp(("parallel",)),
        name="rmsnorm",
    )(x, g.reshape(1, d))


def _proj_in_body(x_ref, w_ref, o_ref):
    o_ref[...] = jnp.dot(x_ref[...], w_ref[...], preferred_element_type=f32)


def _proj_in(xn, w, tm=1088, tn=1024):
    m, k = xn.shape
    n = w.shape[1]
    return pl.pallas_call(
        _proj_in_body,
        grid=(n // tn, m // tm),
        in_specs=[pl.BlockSpec((tm, k), lambda j, i: (i, 0)), pl.BlockSpec((k, tn), lambda j, i: (0, j))],
        out_specs=pl.BlockSpec((tm, tn), lambda j, i: (i, j)),
        out_shape=jax.ShapeDtypeStruct((m, n), f32),
        compiler_params=_cp(("parallel", "parallel")),
        name="proj_in",
    )(xn, w)


def _branch_body(ya_ref, yb_ref, yc_ref, ga_ref, gb_ref, gc_ref, wa_ref, wb_ref, wc_ref, o_ref):
    d = lambda y, w: jnp.dot(y[...], w[...], preferred_element_type=f32)
    acc = (_sigmoid(ga_ref[...]) * d(ya_ref, wa_ref) + _sigmoid(gb_ref[...]) * d(yb_ref, wb_ref)
           + _sigmoid(gc_ref[...]) * d(yc_ref, wc_ref))
    o_ref[...] = acc.astype(o_ref.dtype)


def _branch_mix(ya, yb, yc, u, wa, wb, wc, tm=512, tn=512):
    m = ya.shape[0]
    gb0 = C_UG // tn
    nb = D_MODEL // tn
    yspec = pl.BlockSpec((tm, MIX), lambda j, i: (i, 0))
    wspec = pl.BlockSpec((MIX, tn), lambda j, i: (0, j))
    gspec = lambda off: pl.BlockSpec((tm, tn), lambda j, i: (i, gb0 + off * nb + j))
    return pl.pallas_call(
        _branch_body,
        grid=(nb, m // tm),
        in_specs=[yspec, yspec, yspec, gspec(0), gspec(1), gspec(2), wspec, wspec, wspec],
        out_specs=pl.BlockSpec((tm, tn), lambda j, i: (i, j)),
        out_shape=jax.ShapeDtypeStruct((m, D_MODEL), bf16),
        compiler_params=_cp(("parallel", "parallel")),
        name="branch_mix",
    )(ya, yb, yc, u, u, u, wa, wb, wc)


def _outproj_body(x_ref, mx_ref, w_ref, g_ref, x1_ref, xn_ref):
    x1 = x_ref[...] + jnp.dot(mx_ref[...], w_ref[...], preferred_element_type=f32)
    x1_ref[...] = x1
    xn = x1 * lax.rsqrt(jnp.mean(x1 * x1, axis=-1, keepdims=True) + EPS) * g_ref[...]
    xn_ref[...] = xn.astype(xn_ref.dtype)


def _outproj(x, mixed, w, g, tm=256):
    m, d = x.shape
    row = pl.BlockSpec((tm, d), lambda i: (i, 0))
    return pl.pallas_call(
        _outproj_body,
        grid=(m // tm,),
        in_specs=[row, row, pl.BlockSpec((d, d), lambda i: (0, 0)), pl.BlockSpec((1, d), lambda i: (0, 0))],
        out_specs=[row, row],
        out_shape=[jax.ShapeDtypeStruct((m, d), f32), jax.ShapeDtypeStruct((m, d), bf16)],
        compiler_params=_cp(("parallel",)),
        name="outproj",
    )(x, mixed, w, g.reshape(1, d))


def _ffn_up_body(x_ref, wg_ref, wu_ref, o_ref, wgb, wub):
    @pl.when(pl.program_id(1) == 0)
    def _():
        wgb[...] = _bf(wg_ref[...])
        wub[...] = _bf(wu_ref[...])

    x = x_ref[...]
    a = jnp.dot(x, wgb[...], preferred_element_type=f32)
    b = jnp.dot(x, wub[...], preferred_element_type=f32)
    o_ref[...] = (_silu(a) * b).astype(o_ref.dtype)


def _ffn_up(xn, wg, wu, tm=512, tn=512):
    m, k = xn.shape
    n = wg.shape[1]
    wspec = pl.BlockSpec((k, tn), lambda j, i: (0, j))
    return pl.pallas_call(
        _ffn_up_body,
        grid=(n // tn, m // tm),
        in_specs=[pl.BlockSpec((tm, k), lambda j, i: (i, 0)), wspec, wspec],
        out_specs=pl.BlockSpec((tm, tn), lambda j, i: (i, j)),
        out_shape=jax.ShapeDtypeStruct((m, n), bf16),
        scratch_shapes=[pltpu.VMEM((k, tn), bf16), pltpu.VMEM((k, tn), bf16)],
        compiler_params=_cp(("parallel", "arbitrary")),
        name="ffn_up",
    )(xn, wg, wu)


def _ffn_down_body(h_ref, w_ref, x_ref, o_ref, wb):
    @pl.when(pl.program_id(1) == 0)
    def _():
        wb[...] = _bf(w_ref[...])

    o_ref[...] = x_ref[...] + jnp.dot(h_ref[...], wb[...], preferred_element_type=f32)


def _ffn_down(h, w, x, tm=256, tn=512):
    m, k = h.shape
    n = w.shape[1]
    return pl.pallas_call(
        _ffn_down_body,
        grid=(n // tn, m // tm),
        in_specs=[pl.BlockSpec((tm, k), lambda j, i: (i, 0)), pl.BlockSpec((k, tn), lambda j, i: (0, j)),
                  pl.BlockSpec((tm, tn), lambda j, i: (i, j))],
        out_specs=pl.BlockSpec((tm, tn), lambda j, i: (i, j)),
        out_shape=jax.ShapeDtypeStruct((m, n), f32),
        scratch_shapes=[pltpu.VMEM((k, tn), bf16)],
        compiler_params=_cp(("parallel", "arbitrary")),
        name="ffn_down",
    )(h, w, x)


def _router_body(x_ref, g_ref, r_ref, ti_ref, tw_ref):
    x = x_ref[...]
    xn = x * lax.rsqrt(jnp.mean(x * x, axis=-1, keepdims=True) + EPS) * g_ref[...]
    logits = jnp.dot(xn, r_ref[...], preferred_element_type=f32, precision=lax.Precision.HIGHEST)
    lane_i = _iota2(logits.shape, 1)
    lane = lane_i.astype(f32)
    lg = jnp.where(lane_i < N_EXP, logits, -jnp.inf)
    v1 = jnp.max(lg, axis=1, keepdims=True)
    i1 = jnp.min(jnp.where(lg == v1, lane, float(LANES)), axis=1, keepdims=True)
    lg2 = jnp.where(lane == i1, -jnp.inf, lg)
    v2 = jnp.max(lg2, axis=1, keepdims=True)
    i2 = jnp.min(jnp.where(lg2 == v2, lane, float(LANES)), axis=1, keepdims=True)
    e2 = jnp.exp(v2 - v1)
    w1 = 1.0 / (1.0 + e2)
    w2 = e2 / (1.0 + e2)
    ti_ref[...] = jnp.where(lane_i == 0, i1, jnp.where(lane_i == 1, i2, 0.0)).astype(jnp.int32)
    tw_ref[...] = jnp.where(lane_i == 0, w1, jnp.where(lane_i == 1, w2, 0.0))


def _router(x, g, router, tm=512):
    m, d = x.shape
    rp = jnp.zeros((d, LANES), f32).at[:, :N_EXP].set(router)
    row = pl.BlockSpec((tm, LANES), lambda i: (i, 0))
    return pl.pallas_call(
        _router_body,
        grid=(m // tm,),
        in_specs=[pl.BlockSpec((tm, d), lambda i: (i, 0)), pl.BlockSpec((1, d), lambda i: (0, 0)),
                  pl.BlockSpec((d, LANES), lambda i: (0, 0))],
        out_specs=[row, row],
        out_shape=[jax.ShapeDtypeStruct((m, LANES), jnp.int32), jax.ShapeDtypeStruct((m, LANES), f32)],
        compiler_params=_cp(("parallel",)),
        name="router",
    )(x, g.reshape(1, d), rp)


def _new_expert(te_ref, i):
    prev = te_ref[jnp.maximum(i - 1, 0)]
    return jnp.logical_or(i == 0, te_ref[i] != prev)


def _moe_up_body(te_ref, tv_ref, x_ref, wg_ref, wu_ref, o_ref, wgb, wub):
    i = pl.program_id(1)

    @pl.when(_new_expert(te_ref, i))
    def _():
        wgb[...] = _bf(wg_ref[0])
        wub[...] = _bf(wu_ref[0])

    @pl.when(tv_ref[i] > 0)
    def _():
        x = x_ref[...]
        a = jnp.dot(x, wgb[...], preferred_element_type=f32)
        b = jnp.dot(x, wub[...], preferred_element_type=f32)
        o_ref[...] = (_silu(a) * b).astype(o_ref.dtype)

    @pl.when(tv_ref[i] == 0)
    def _():
        o_ref[...] = jnp.zeros(o_ref.shape, o_ref.dtype)


def _moe_up(xg, wg, wu, te, tv, tm, tn=256):
    p, k = xg.shape
    n = wg.shape[2]
    wspec = pl.BlockSpec((1, k, tn), lambda j, i, te, tv: (te[i], 0, j))
    return pl.pallas_call(
        _moe_up_body,
        grid_spec=pltpu.PrefetchScalarGridSpec(
            num_scalar_prefetch=2,
            grid=(n // tn, p // tm),
            in_specs=[pl.BlockSpec((tm, k), lambda j, i, te, tv: (i, 0)), wspec, wspec],
            out_specs=pl.BlockSpec((tm, tn), lambda j, i, te, tv: (i, j)),
            scratch_shapes=[pltpu.VMEM((k, tn), bf16), pltpu.VMEM((k, tn), bf16)],
        ),
        out_shape=jax.ShapeDtypeStruct((p, n), bf16),
        compiler_params=_cp(("parallel", "arbitrary")),
        name="moe_up",
    )(te, tv, xg, wg, wu)


def _moe_down_body(te_ref, tv_ref, h_ref, w_ref, rw_ref, o_ref, wb):
    i = pl.program_id(1)

    @pl.when(_new_expert(te_ref, i))
    def _():
        wb[...] = _bf(w_ref[0])

    @pl.when(tv_ref[i] > 0)
    def _():
        o_ref[...] = jnp.dot(h_ref[...], wb[...], preferred_element_type=f32) * rw_ref[:, 0:1]

    @pl.when(tv_ref[i] == 0)
    def _():
        o_ref[...] = jnp.zeros(o_ref.shape, o_ref.dtype)


def _moe_down(hg, wd, roww, te, tv, tm, tn=512):
    p, k = hg.shape
    n = wd.shape[2]
    return pl.pallas_call(
        _moe_down_body,
        grid_spec=pltpu.PrefetchScalarGridSpec(
            num_scalar_prefetch=2,
            grid=(n // tn, p // tm),
            in_specs=[pl.BlockSpec((tm, k), lambda j, i, te, tv: (i, 0)),
                      pl.BlockSpec((1, k, tn), lambda j, i, te, tv: (te[i], 0, j)),
                      pl.BlockSpec((tm, LANES), lambda j, i, te, tv: (i, 0))],
            out_specs=pl.BlockSpec((tm, tn), lambda j, i, te, tv: (i, j)),
            scratch_shapes=[pltpu.VMEM((k, tn), bf16)],
        ),
        out_shape=jax.ShapeDtypeStruct((p, n), f32),
        compiler_params=_cp(("parallel", "arbitrary")),
        name="moe_down",
    )(te, tv, hg, wd, roww)


def _moe(x1, xn_bf, g, router, wg, wu, wd, tm=512):
    m = x1.shape[0]
    top_i, top_w = _router(x1, g, router)
    e_flat = top_i[:, :2].reshape(-1)
    w_flat = top_w[:, :2].reshape(-1)
    npair = 2 * m
    ntile = npair // tm + N_EXP
    ptot = ntile * tm
    order = jnp.argsort(e_flat, stable=True)
    counts = jnp.sum(jax.nn.one_hot(e_flat, N_EXP, dtype=jnp.int32), axis=0)
    tiles_per = (counts + tm - 1) // tm
    tile_end = jnp.cumsum(tiles_per)
    gstart = (tile_end - tiles_per) * tm
    cstart = jnp.cumsum(counts) - counts
    e_sorted = e_flat[order]
    rank = jnp.arange(npair, dtype=jnp.int32) - cstart[e_sorted]
    dest = gstart[e_sorted] + rank
    src_tok = jnp.zeros((ptot,), jnp.int32).at[dest].set((order // 2).astype(jnp.int32))
    roww = jnp.zeros((ptot,), f32).at[dest].set(w_flat[order])
    pos = jnp.zeros((npair,), jnp.int32).at[order].set(dest.astype(jnp.int32))
    tidx = jnp.arange(ntile, dtype=jnp.int32)
    te = jnp.minimum(jnp.sum((tidx[:, None] >= tile_end[None, :]).astype(jnp.int32), axis=1), N_EXP - 1)
    tv = (tidx < tile_end[-1]).astype(jnp.int32)
    te = jnp.where(tv > 0, te, te[jnp.maximum(tile_end[-1] - 1, 0)]).astype(jnp.int32)

    xg = jnp.take(xn_bf, src_tok, axis=0)
    hg = _moe_up(xg, wg, wu, te, tv, tm)
    yg = _moe_down(hg, wd, jnp.broadcast_to(roww[:, None], (ptot, LANES)), te, tv, tm)
    pos2 = pos.reshape(m, 2)
    return x1 + jnp.take(yg, pos2[:, 0], axis=0) + jnp.take(yg, pos2[:, 1], axis=0)


def _mlstm_body(*refs, L, t_valid, n_in):
    q_ref, k_ref, v_ref, o_ref, g_ref, bias_ref, ng_ref, c0_ref, n0_ref, m0_ref = refs[:10]
    y_ref, c_ref, n_ref, m_ref = refs[n_in:n_in + 4]
    c = pl.program_id(1)

    @pl.when(c == 0)
    def _():
        c_ref[...] = c0_ref[...]
        n_ref[...] = n0_ref[...]
        m_ref[...] = m0_ref[...]

    tril_b, tril, triu, eye = _tri_masks(L)
    tg = GATE_CAP * jnp.tanh((g_ref[...] + bias_ref[...]) / GATE_CAP)
    li_all = tg
    lf_all = jnp.minimum(tg, 0.0) - jnp.log1p(jnp.exp(-jnp.abs(tg)))
    if t_valid is not None:
        valid = (c * L + _iota2((L, LANES), 0)) < t_valid
        li_all = jnp.where(valid, li_all, NEG_BIG)
        lf_all = jnp.where(valid, lf_all, 0.0)
    b_all = _dot3(tril, lf_all)
    b_rows = _dot3_tn(lf_all, triu)
    li_rows = _dot3_tn(li_all, eye)

    for h in range(H_A):
        q = q_ref[:, h * DK_A:(h + 1) * DK_A]
        k = k_ref[:, h * DK_A:(h + 1) * DK_A] * (DK_A ** -0.5)
        v = v_ref[:, h * DV_A:(h + 1) * DV_A]
        bc = b_all[:, GL_F + h:GL_F + h + 1]
        br = b_rows[GL_F + h:GL_F + h + 1, :]
        lr = li_rows[GL_I + h:GL_I + h + 1, :]
        lc = li_all[:, GL_I + h:GL_I + h + 1]
        m_prev = m_ref[0, h:h + 1, 0:1]
        c_prev = c_ref[h]
        n_prev = n_ref[0, h:h + 1, :]

        dmat = jnp.where(tril_b, bc - br + lr, -jnp.inf)
        g_inter = bc + m_prev
        m_t = jnp.maximum(g_inter, jnp.max(dmat, axis=1, keepdims=True))
        w_inter = jnp.exp(g_inter - m_t)
        s = _dot_nt(q, k) * jnp.exp(dmat - m_t)
        num = w_inter * _dot(q, c_prev) + _dot(s, v)
        den = w_inter * jnp.sum(q * n_prev, axis=1, keepdims=True) + jnp.sum(s, axis=1, keepdims=True)
        hh = num / jnp.maximum(jnp.abs(den), jnp.exp(-m_t))

        b_last = bc[L - 1:L, :]
        g_s = b_last - bc + lc
        m_new = jnp.maximum(b_last + m_prev, jnp.max(g_s, axis=0, keepdims=True))
        a_s = jnp.exp(g_s - m_new)
        decay = jnp.exp(b_last + m_prev - m_new)
        ak = a_s * k
        c_ref[h] = decay * c_prev + _dot_tn(ak, v)
        n_ref[0, h:h + 1, :] = decay * n_prev + jnp.sum(ak, axis=0, keepdims=True)
        m_ref[0, h:h + 1, :] = jnp.broadcast_to(m_new, (1, LANES))

        hn = hh * lax.rsqrt(jnp.mean(hh * hh, axis=-1, keepdims=True) + EPS) * ng_ref[:, h * DV_A:(h + 1) * DV_A]
        y = hn * _sigmoid(o_ref[:, h * DV_A:(h + 1) * DV_A])
        y_ref[:, h * DV_A:(h + 1) * DV_A] = y.astype(y_ref.dtype)


def _state_io(tail, st, acc):
    zeros = (0,) * len(tail)
    in_spec = pl.BlockSpec((None, None) + tail, lambda b, c: (st.li, b) + zeros)
    out_spec = pl.BlockSpec((None, None) + tail, lambda b, c: (st.lo, b) + zeros)
    extra_in = [] if acc is None else [pl.BlockSpec(memory_space=pl.ANY)]
    extra_args = [] if acc is None else [acc]
    return in_spec, out_spec, extra_in, extra_args


class _St:
    def __init__(self, arr, li, lo, depth):
        self.arr, self.li, self.lo, self.depth = arr, li, lo, depth


def _mlstm(u, bias_row, norm_g, st, acc, n0, m0, *, bt, t, L, t_valid):
    nc = t // L
    rb = lambda b, c: b * nc + c
    m0b = jnp.broadcast_to(m0[:, :, None], (bt, H_A, LANES))
    c_in, c_out, extra_in, extra_args = _state_io((H_A, DK_A, DV_A), st, acc)
    in_specs = [
        pl.BlockSpec((L, 512), lambda b, c: (rb(b, c), C_Q // 512)),
        pl.BlockSpec((L, 512), lambda b, c: (rb(b, c), C_K // 512)),
        pl.BlockSpec((L, MIX), lambda b, c: (rb(b, c), C_V // MIX)),
        pl.BlockSpec((L, MIX), lambda b, c: (rb(b, c), C_O // MIX)),
        pl.BlockSpec((L, LANES), lambda b, c: (rb(b, c), C_GATE // LANES)),
        pl.BlockSpec((1, LANES), lambda b, c: (0, 0)),
        pl.BlockSpec((1, MIX), lambda b, c: (0, 0)),
        c_in,
        pl.BlockSpec((1, H_A, DK_A), lambda b, c: (b, 0, 0)),
        pl.BlockSpec((1, H_A, LANES), lambda b, c: (b, 0, 0)),
    ] + extra_in
    out_specs = [
        pl.BlockSpec((L, MIX), lambda b, c: (rb(b, c), 0)),
        c_out,
        pl.BlockSpec((1, H_A, DK_A), lambda b, c: (b, 0, 0)),
        pl.BlockSpec((1, H_A, LANES), lambda b, c: (b, 0, 0)),
    ]
    out_shape = [
        jax.ShapeDtypeStruct((bt * t, MIX), bf16 if L % 16 == 0 else f32),
        jax.ShapeDtypeStruct((st.depth, bt, H_A, DK_A, DV_A), f32),
        jax.ShapeDtypeStruct((bt, H_A, DK_A), f32),
        jax.ShapeDtypeStruct((bt, H_A, LANES), f32),
    ]
    y, c_new, n_new, m_new = pl.pallas_call(
        functools.partial(_mlstm_body, L=L, t_valid=t_valid, n_in=len(in_specs)),
        grid=(bt, nc), in_specs=in_specs, out_specs=out_specs, out_shape=out_shape,
        input_output_aliases={} if acc is None else {len(in_specs) - 1: 1},
        compiler_params=_cp(("parallel", "arbitrary")), name="mlstm",
    )(u, u, u, u, u, bias_row, norm_g.reshape(1, MIX), st.arr, n0, m0b, *extra_args)
    return y, c_new, n_new, m_new[:, :, 0]


def _ssd_body(*refs, L, t_valid, n_in):
    (z_ref, xs_ref, bcr_ref, g_ref, bias_ref, cwx_ref, cwb_ref, cbx_ref, cbb_ref, alog_ref,
     d_ref, ng_ref, cx0_ref, cb0_ref, h0_ref) = refs[:15]
    y_ref, h_ref, px_sc, pb_sc = refs[n_in:n_in + 4]
    c = pl.program_id(1)

    @pl.when(c == 0)
    def _():
        h_ref[...] = h0_ref[...]
        px_sc[...] = jnp.zeros(px_sc.shape, f32)
        pb_sc[...] = jnp.zeros(pb_sc.shape, f32)
        px_sc[L - 8:L, :] = cx0_ref[0]
        pb_sc[L - 8:L, :] = cb0_ref[0]

    def conv(cur, prev, cw_ref, cb_ref):
        row = _iota2(cur.shape, 0)
        acc = cb_ref[...] + cw_ref[CONV_W - 1:CONV_W, :] * cur
        for sft in range(1, CONV_W):
            shifted = jnp.where(row >= sft, pltpu.roll(cur, sft, 0), pltpu.roll(prev, sft, 0))
            acc = acc + cw_ref[CONV_W - 1 - sft:CONV_W - sft, :] * shifted
        return _silu(acc)

    cur_x = xs_ref[...]
    cur_b = bcr_ref[...]
    xs = conv(cur_x, px_sc[...], cwx_ref, cbx_ref)
    bcm = conv(cur_b, pb_sc[...], cwb_ref, cbb_ref)
    px_sc[...] = cur_x
    pb_sc[...] = cur_b

    tril_b, tril, triu, eye = _tri_masks(L)
    dt_all = _softplus(g_ref[...] + bias_ref[...])
    if t_valid is not None:
        valid = (c * L + _iota2((L, LANES), 0)) < t_valid
        dt_all = jnp.where(valid, dt_all, 0.0)
    da_all = dt_all * (-jnp.exp(alog_ref[...]))
    acs_all = _dot3(tril, da_all)
    acs_rows = _dot3_tn(da_all, triu)
    dt_rows = _dot3_tn(dt_all, eye)

    lane0 = _iota2((L, LANES), 1) < P_B
    row0 = _iota2((2 * P_B, 1), 0) < P_B
    npair = H_B // 2
    ys = []
    for p in range(npair):
        grp = (2 * p) // (H_B // G_B)
        bm = bcm[:, grp * N_B:(grp + 1) * N_B]
        cm = bcm[:, G_B * N_B + grp * N_B:G_B * N_B + (grp + 1) * N_B]
        cb = _dot_nt(cm, bm)
        xp = xs[:, p * LANES:(p + 1) * LANES]
        wts, acs, tails, alast = [], [], [], []
        for hd in (2 * p, 2 * p + 1):
            ln = GL_DT + hd
            ac = acs_all[:, ln:ln + 1]
            ar = acs_rows[ln:ln + 1, :]
            seg = jnp.where(tril_b, ac - ar, -jnp.inf)
            wts.append(cb * jnp.exp(seg) * dt_rows[ln:ln + 1, :])
            al = ac[L - 1:L, :]
            acs.append(ac)
            alast.append(al)
            tails.append(jnp.exp(al - ac) * dt_all[:, ln:ln + 1])
        hp = h_ref[2 * p:2 * p + 2].reshape(2 * P_B, N_B)
        yp = jnp.where(lane0, _dot(wts[0], xp), _dot(wts[1], xp))
        yp = yp + _dot_nt(cm, hp) * jnp.where(lane0, jnp.exp(acs[0]), jnp.exp(acs[1]))
        tailx = xp * jnp.where(lane0, tails[0], tails[1])
        dec = jnp.where(row0, jnp.exp(alast[0]), jnp.exp(alast[1]))
        h_ref[2 * p:2 * p + 2] = (dec * hp + _dot_tn(tailx, bm)).reshape(2, P_B, N_B)
        ys.append(yp + d_ref[:, p * LANES:(p + 1) * LANES] * xp)

    y = jnp.concatenate(ys, axis=1) * _silu(z_ref[...])
    gw = MIX // G_B
    outs = []
    for grp in range(G_B):
        sg = y[:, grp * gw:(grp + 1) * gw]
        outs.append(sg * lax.rsqrt(jnp.mean(sg * sg, axis=-1, keepdims=True) + EPS))
    y_ref[...] = (jnp.concatenate(outs, axis=1) * ng_ref[...]).astype(y_ref.dtype)


def _ssd(u, bias_row, conv_w, conv_b, a_log, d_skip, norm_g, conv0, st, acc, *, bt, t, L, t_valid):
    nc = t // L
    rb = lambda b, c: b * nc + c
    const = lambda shape: pl.BlockSpec(shape, lambda b, c: tuple(0 for _ in shape))
    alog_row = jnp.zeros((1, LANES), f32).at[0, GL_DT:GL_DT + H_B].set(a_log)
    d_row = jnp.repeat(d_skip, P_B).reshape(1, MIX)
    conv0p = jnp.pad(conv0, ((0, 0), (8 - (CONV_W - 1), 0), (0, 0)))
    h_in, h_out, extra_in, extra_args = _state_io((H_B, P_B, N_B), st, acc)
    in_specs = [
        pl.BlockSpec((L, MIX), lambda b, c: (rb(b, c), C_Z // MIX)),
        pl.BlockSpec((L, MIX), lambda b, c: (rb(b, c), C_XS // MIX)),
        pl.BlockSpec((L, 512), lambda b, c: (rb(b, c), C_BC // 512)),
        pl.BlockSpec((L, LANES), lambda b, c: (rb(b, c), C_GATE // LANES)),
        const((1, LANES)), const((CONV_W, MIX)), const((CONV_W, 512)), const((1, MIX)), const((1, 512)),
        const((1, LANES)), const((1, MIX)), const((1, MIX)),
        pl.BlockSpec((1, 8, MIX), lambda b, c: (b, 0, 0)),
        pl.BlockSpec((1, 8, 512), lambda b, c: (b, 0, 0)),
        h_in,
    ] + extra_in
    out_specs = [pl.BlockSpec((L, MIX), lambda b, c: (rb(b, c), 0)), h_out]
    out_shape = [jax.ShapeDtypeStruct((bt * t, MIX), bf16 if L % 16 == 0 else f32),
                 jax.ShapeDtypeStruct((st.depth, bt, H_B, P_B, N_B), f32)]
    return pl.pallas_call(
        functools.partial(_ssd_body, L=L, t_valid=t_valid, n_in=len(in_specs)),
        grid=(bt, nc), in_specs=in_specs, out_specs=out_specs, out_shape=out_shape,
        input_output_aliases={} if acc is None else {len(in_specs) - 1: 1},
        scratch_shapes=[pltpu.VMEM((L, MIX), f32), pltpu.VMEM((L, 512), f32)],
        compiler_params=_cp(("parallel", "arbitrary")), name="ssd",
    )(u, u, u, u, bias_row, conv_w[:, :MIX], conv_w[:, MIX:], conv_b[:MIX].reshape(1, MIX),
      conv_b[MIX:].reshape(1, 512), alog_row, d_row, norm_g.reshape(1, MIX),
      conv0p[:, :, :MIX], conv0p[:, :, MIX:], st.arr, *extra_args)


PR_MU_R, PR_MU_K, PR_MU_V, PR_W0, PR_A0, PR_KK, PR_KA, PR_RK, PR_LNW, PR_LNB = range(10)


def _rwkv_body(*refs, R, nslot, t_valid, n_in):
    (r_ref, k_ref, v_ref, l_ref, par_ref, mul_ref, w2_ref, a2_ref, g2_ref,
     shr_ref, shk_ref, shv_ref, shl_ref, s0_ref) = refs[:14]
    y_ref, s_ref, cr_sc, ck_sc, cv_sc, cl_sc, st_sc = refs[n_in:n_in + 7]
    c = pl.program_id(1)
    nc = pl.num_programs(1)
    npair = H_C // 2
    lsub = R // nslot
    lg = lsub.bit_length() - 1
    L2 = 2 * R

    @pl.when(c == 0)
    def _():
        if nslot == 1:
            cr_sc[...] = shr_ref[...]
            ck_sc[...] = shk_ref[...]
            cv_sc[...] = shv_ref[...]
            cl_sc[...] = shl_ref[...]
        rowm = _iota2((2 * K_C, K_C), 0) < K_C
        for s in range(nslot):
            for p in range(npair):
                x = s0_ref[s, 2 * p:2 * p + 2].reshape(2 * K_C, K_C)
                st_sc[s * npair + p] = jnp.concatenate([jnp.where(rowm, x, 0.0), jnp.where(rowm, 0.0, x)], axis=1)

    par = lambda i: par_ref[i:i + 1, :]

    def shifted(cur_ref, carry, sh_ref, mu):
        cur = cur_ref[...]
        first = (_iota2(cur.shape, 0) & (lsub - 1)) == 0
        if nslot == 1:
            before = carry[0:1, :]
            carry[0:1, :] = cur[R - 1:R, :]
        else:
            before = sh_ref[...]
        prev = jnp.where(first, before, pltpu.roll(cur, 1, 0))
        return cur + (prev - cur) * mu

    xr = shifted(r_ref, cr_sc, shr_ref, par(PR_MU_R))
    xk = shifted(k_ref, ck_sc, shk_ref, par(PR_MU_K))
    xv = shifted(v_ref, cv_sc, shv_ref, par(PR_MU_V))
    xl = shifted(l_ref, cl_sc, shl_ref, mul_ref[...])

    lw = jnp.dot(_bf(jnp.tanh(xl)), w2_ref[...], preferred_element_type=f32)
    la = jnp.dot(_bf(xl), a2_ref[...], preferred_element_type=f32)
    g = jnp.dot(_bf(_sigmoid(xl)), g2_ref[...], preferred_element_type=f32)
    wlog = -_softplus(-(par(PR_W0) + lw)) - 0.5
    logdec = -jnp.exp(wlog)
    a = _sigmoid(par(PR_A0) + la)
    kkr = xk * par(PR_KK)
    k2 = xk * (1.0 + (a - 1.0) * par(PR_KA))
    if t_valid is not None:
        valid = (_iota2((R, MIX), 0) & (lsub - 1)) < t_valid
        logdec = jnp.where(valid, logdec, 0.0)
        kkr = jnp.where(valid, kkr, 0.0)
        k2 = jnp.where(valid, k2, 0.0)
        xv = jnp.where(valid, xv, 0.0)

    rr = _iota2((R, R), 0)
    cc = _iota2((R, R), 1)
    tril = jnp.where(jnp.logical_and(cc <= rr, (rr >> lg) == (cc >> lg)), 1.0, 0.0).astype(bf16)
    cum = _dot3(tril, logdec)
    pfull = jnp.exp(cum)
    pinv = jnp.exp(-cum)
    pprev = jnp.exp(cum - logdec)

    lane0 = _iota2((R, LANES), 1) < K_C
    r2 = _iota2((L2, L2), 0)
    c2 = _iota2((L2, L2), 1)
    same = (r2 >> lg) == (c2 >> lg)
    strict = jnp.logical_and(same, (c2 & (lsub - 1)) < (r2 & (lsub - 1)))
    incl = jnp.logical_and(same, (c2 & (lsub - 1)) <= (r2 & (lsub - 1)))
    eye2 = jnp.where(r2 == c2, 1.0, 0.0)

    def hsum(x):
        s0 = jnp.sum(jnp.where(lane0, x, 0.0), axis=1, keepdims=True)
        s1 = jnp.sum(jnp.where(lane0, 0.0, x), axis=1, keepdims=True)
        return jnp.where(lane0, s0, s1)

    def stack(x):
        h0 = jnp.where(lane0, x, 0.0)
        h1 = jnp.where(lane0, 0.0, x)
        pieces = []
        for s in range(nslot):
            pieces += [h0[s * lsub:(s + 1) * lsub], h1[s * lsub:(s + 1) * lsub]]
        return jnp.concatenate(pieces, axis=0)

    nsq = max(lg - 1, 0)
    w = 2 * lsub
    pairs = range(npair)
    sls = [slice(p * LANES, (p + 1) * LANES) for p in pairs]
    s_old = [[st_sc[s * npair + p] for s in range(nslot)] for p in pairs]
    kap2, bt2, kt2, rt2, v2 = [], [], [], [], []
    for p in pairs:
        sl = sls[p]
        kkp = kkr[:, sl]
        kap = kkp / jnp.maximum(jnp.sqrt(hsum(kkp * kkp)), 1e-12)
        kap2.append(_bf(stack(kap * pprev[:, sl])))
        bt2.append(stack(kap * a[:, sl] * pinv[:, sl]))
        kt2.append(stack(k2[:, sl] * pinv[:, sl]))
        rt2.append(_bf(stack(xr[:, sl] * pfull[:, sl])))
        v2.append(_bf(stack(xv[:, sl])))
    gram = [_dot_nt(jnp.concatenate([kap2[p], rt2[p]], axis=0),
                    jnp.concatenate([_bf(bt2[p]), _bf(kt2[p])], axis=0)) for p in pairs]
    n_kb = [jnp.where(strict, gram[p][0:L2, 0:L2], 0.0) for p in pairs]
    a_kr = [jnp.concatenate([jnp.where(strict, gram[p][0:L2, L2:2 * L2], 0.0),
                             jnp.where(incl, gram[p][L2:2 * L2, L2:2 * L2], 0.0)], axis=0) for p in pairs]
    a_rb = [jnp.where(incl, gram[p][L2:2 * L2, 0:L2], 0.0) for p in pairs]
    tinv = [eye2 - n_kb[p] for p in pairs]
    if nsq > 0:
        pw = [_dot(n_kb[p], n_kb[p]) for p in pairs]
        for it in range(nsq):
            if it + 1 < nsq:
                both = [_dot(jnp.concatenate([pw[p], tinv[p]], axis=0), pw[p]) for p in pairs]
                pw = [both[p][0:L2] for p in pairs]
                tinv = [tinv[p] + both[p][L2:2 * L2] for p in pairs]
            else:
                tinv = [tinv[p] + _dot(tinv[p], pw[p]) for p in pairs]
    ks = [[_dot_nt(jnp.concatenate([kap2[p][s * w:(s + 1) * w], rt2[p][s * w:(s + 1) * w]], axis=0), s_old[p][s])
           for s in range(nslot)] for p in pairs]
    av = [_dot(a_kr[p], v2[p]) for p in pairs]
    u2 = [_dot(tinv[p], jnp.concatenate([z[0:w] for z in ks[p]], axis=0) + av[p][0:L2]) for p in pairs]
    au = [_dot(a_rb[p], u2[p]) for p in pairs]
    for p in pairs:
        u2b = _bf(u2[p])
        for s in range(nslot):
            pl_row = pfull[(s + 1) * lsub - 1:(s + 1) * lsub, sls[p]]
            lhs = jnp.concatenate([v2[p][s * w:(s + 1) * w], -u2b[s * w:(s + 1) * w]], axis=0)
            rhs = jnp.concatenate([kt2[p][s * w:(s + 1) * w], bt2[p][s * w:(s + 1) * w]], axis=0) * pl_row
            st_sc[s * npair + p] = s_old[p][s] * pl_row + _dot_tn(lhs, rhs)
    for p in pairs:
        sl = sls[p]
        y2 = jnp.concatenate([z[w:2 * w] for z in ks[p]], axis=0) + av[p][L2:2 * L2] - au[p]
        yp = jnp.concatenate([y2[s * w:s * w + lsub] + y2[s * w + lsub:(s + 1) * w] for s in range(nslot)], axis=0)
        mean = hsum(yp) * (1.0 / K_C)
        dy = yp - mean
        var = hsum(dy * dy) * (1.0 / K_C)
        yn = dy * lax.rsqrt(var + LN_EPS_C) * par(PR_LNW)[:, sl] + par(PR_LNB)[:, sl]
        bonus = hsum(xr[:, sl] * k2[:, sl] * par(PR_RK)[:, sl]) * xv[:, sl]
        y_ref[:, sl] = ((yn + bonus) * g[:, sl]).astype(y_ref.dtype)

    @pl.when(c == nc - 1)
    def _():
        for s in range(nslot):
            for p in range(npair):
                blk = st_sc[s * npair + p]
                s_ref[s, 2 * p] = blk[0:K_C, 0:K_C]
                s_ref[s, 2 * p + 1] = blk[K_C:2 * K_C, K_C:2 * K_C]


RWKV_ROWS = 64


def _rwkv(u, par, mu_l, w2p, a2p, g2p, shift0, st, acc, *, bt, t, t_valid):
    R = RWKV_ROWS
    nslot = 1 if t >= R else R // t
    nc = max(t // R, 1)
    nblk = bt // nslot
    rb = lambda b, c: b * nc + c
    const = lambda shape: pl.BlockSpec(shape, lambda b, c: tuple(0 for _ in shape))
    npair = H_C // 2
    lsub = R // nslot
    sh_rows = 8 if nslot == 1 else lsub
    padr = lambda x: jnp.pad(x[:, None, :], ((0, 0), (0, sh_rows - 1), (0, 0))).reshape(bt * sh_rows, x.shape[-1])
    sh_r = padr(shift0[:, 0:MIX])
    sh_k = padr(shift0[:, MIX:2 * MIX])
    sh_v = padr(shift0[:, 2 * MIX:3 * MIX])
    sh_l = padr(jnp.pad(shift0[:, 3 * MIX:], ((0, 0), (0, 512 - (R_W - 3 * MIX)))))
    shb = 8 if nslot == 1 else R
    tail = (H_C, K_C, K_C)
    zeros = (0,) * len(tail)
    s_in = pl.BlockSpec((None, nslot) + tail, lambda b, c: (st.li, b) + zeros)
    s_out = pl.BlockSpec((None, nslot) + tail, lambda b, c: (st.lo, b) + zeros)
    extra_in = [] if acc is None else [pl.BlockSpec(memory_space=pl.ANY)]
    extra_args = [] if acc is None else [acc]
    in_specs = [
        pl.BlockSpec((R, MIX), lambda b, c: (rb(b, c), C_R // MIX)),
        pl.BlockSpec((R, MIX), lambda b, c: (rb(b, c), C_KC // MIX)),
        pl.BlockSpec((R, MIX), lambda b, c: (rb(b, c), C_VC // MIX)),
        pl.BlockSpec((R, 512), lambda b, c: (rb(b, c), C_LORA // 512)),
        const((16, MIX)), const((1, 512)), const((512, MIX)), const((512, MIX)), const((512, MIX)),
        pl.BlockSpec((shb, MIX), lambda b, c: (b, 0)),
        pl.BlockSpec((shb, MIX), lambda b, c: (b, 0)),
        pl.BlockSpec((shb, MIX), lambda b, c: (b, 0)),
        pl.BlockSpec((shb, 512), lambda b, c: (b, 0)),
        s_in,
    ] + extra_in
    out_specs = [pl.BlockSpec((R, MIX), lambda b, c: (rb(b, c), 0)), s_out]
    out_shape = [jax.ShapeDtypeStruct((bt * t, MIX), bf16), jax.ShapeDtypeStruct((st.depth, bt) + tail, f32)]
    return pl.pallas_call(
        functools.partial(_rwkv_body, R=R, nslot=nslot, t_valid=t_valid, n_in=len(in_specs)),
        grid=(nblk, nc), in_specs=in_specs, out_specs=out_specs, out_shape=out_shape,
        input_output_aliases={} if acc is None else {len(in_specs) - 1: 1},
        scratch_shapes=[pltpu.VMEM((8, MIX), f32), pltpu.VMEM((8, MIX), f32), pltpu.VMEM((8, MIX), f32),
                        pltpu.VMEM((8, 512), f32), pltpu.VMEM((nslot * npair, 2 * K_C, 2 * K_C), f32)],
        compiler_params=_cp(("parallel", "arbitrary")), name="rwkv",
    )(u, u, u, u, par, mu_l, w2p, a2p, g2p, sh_r, sh_k, sh_v, sh_l, st.arr, *extra_args)


def _permute_w_in(w):
    ob, oc, og = A_COLS, A_COLS + B_COLS, A_COLS + B_COLS + R_W
    z = lambda n: jnp.zeros((w.shape[0], n), w.dtype)
    segs = [
        w[:, 0:3072],
        w[:, ob:ob + MIX],
        w[:, og:og + 3 * D_MODEL],
        w[:, oc:oc + 3 * MIX],
        w[:, ob + MIX:ob + MIX + CONV_DIM],
        w[:, oc + 3 * MIX:oc + R_W], z(512 - (R_W - 3 * MIX)),
        w[:, 3072:3072 + 2 * H_A], w[:, ob + MIX + CONV_DIM:ob + B_COLS], z(LANES - 2 * H_A - H_B),
    ]
    out = jnp.concatenate(segs, axis=1)
    return jnp.pad(out, ((0, 0), (0, P_PAD - out.shape[1]))).astype(bf16)


def _pad_rows(w, before, total):
    return jnp.pad(w, ((before, total - before - w.shape[0]), (0, 0))).astype(bf16)


def kernel(x_prompt, x_sample, state_mlstm_C, state_mlstm_n, state_mlstm_m, state_ssm, state_conv, state_wkv, state_shift, norm_mix, w_in, mlstm_i_bias, mlstm_f_bias, mlstm_norm, mamba_conv_w, mamba_conv_b, mamba_dt_bias, mamba_A_log, mamba_D, mamba_norm, rwkv_mu, rwkv_w0, rwkv_w2, rwkv_a0, rwkv_a2, rwkv_g2, rwkv_k_k, rwkv_k_a, rwkv_r_k, rwkv_ln_w, rwkv_ln_b, w_branch_a, w_branch_b, w_branch_c, w_out, norm_ffn, ffn_w_gate, ffn_w_up, ffn_w_down, moe_router, moe_w_gate, moe_w_up, moe_w_down, norm_final):
    nb, seq, d = x_prompt.shape
    db, dseq, _ = x_sample.shape
    depth = w_in.shape[0]
    mp = nb * seq
    ms = db * dseq
    t_s = 8
    assert seq >= CONV_W - 1 and dseq >= CONV_W - 1 and dseq <= t_s

    x = jnp.concatenate([x_prompt.reshape(mp, d), x_sample.reshape(ms, d)], axis=0)
    xn = _rmsnorm(x, norm_mix[0], bf16)

    zeros_like_state = lambda s: jnp.zeros((nb,) + s.shape[2:], s.dtype)
    new_p = [[] for _ in range(7)]
    new_s = [[] for _ in range(7)]
    acc_s = (None, None, None)

    for l in range(depth):
        u = _proj_in(xn, _permute_w_in(w_in[l]))
        u_s = jnp.pad(u[mp:].reshape(db, dseq, P_PAD), ((0, 0), (0, t_s - dseq), (0, 0))).reshape(db * t_s, P_PAD)

        bias_row = jnp.zeros((1, LANES), f32)
        bias_row = bias_row.at[0, GL_I:GL_I + H_A].set(mlstm_i_bias[l]).at[0, GL_F:GL_F + H_A].set(mlstm_f_bias[l])
        bias_row = bias_row.at[0, GL_DT:GL_DT + H_B].set(mamba_dt_bias[l])

        par = jnp.zeros((16, MIX), f32)
        mu = rwkv_mu[l]
        for idx, val in ((PR_MU_R, mu[0:MIX]), (PR_MU_K, mu[MIX:2 * MIX]), (PR_MU_V, mu[2 * MIX:3 * MIX]),
                         (PR_W0, rwkv_w0[l]), (PR_A0, rwkv_a0[l]), (PR_KK, rwkv_k_k[l]), (PR_KA, rwkv_k_a[l]),
                         (PR_RK, rwkv_r_k[l].reshape(MIX)), (PR_LNW, rwkv_ln_w[l]), (PR_LNB, rwkv_ln_b[l])):
            par = par.at[idx].set(val)
        mu_l = jnp.pad(mu[3 * MIX:], (0, 512 - (R_W - 3 * MIX))).reshape(1, 512)
        w2p = _pad_rows(rwkv_w2[l], 0, 512)
        a2p = _pad_rows(rwkv_a2[l], LORA_W, 512)
        g2p = _pad_rows(rwkv_g2[l], LORA_W + LORA_A, 512)

        def mixers(uu, bt, t, t_valid, la, lb, big, accs, small):
            mn, mm, conv, shift = small
            ya, c_, n_, m_ = _mlstm(uu, bias_row, mlstm_norm[l], big[0], accs[0], mn, mm,
                                    bt=bt, t=t, L=la, t_valid=t_valid)
            yb, h_ = _ssd(uu, bias_row, mamba_conv_w[l], mamba_conv_b[l], mamba_A_log[l], mamba_D[l],
                          mamba_norm[l], conv, big[1], accs[1], bt=bt, t=t, L=lb, t_valid=t_valid)
            yc, s_ = _rwkv(uu, par, mu_l, w2p, a2p, g2p, shift, big[2], accs[2], bt=bt, t=t, t_valid=t_valid)
            return ya, yb, yc, (c_, h_, s_), (n_, m_)

        big_states = (state_mlstm_C, state_ssm, state_wkv)
        zero_big = tuple(_St(jnp.zeros((1, nb) + s.shape[2:], s.dtype), 0, 0, 1) for s in big_states)
        zero_small = tuple(zeros_like_state(s) for s in (state_mlstm_n, state_mlstm_m, state_conv, state_shift))
        ya_p, yb_p, yc_p, big_p, small_p = mixers(u, nb, seq, None, 128, 128, zero_big, (None, None, None),
                                                  zero_small)
        samp_big = tuple(_St(s, l, l, depth) for s in big_states)
        ya_s, yb_s, yc_s, acc_s, small_s = mixers(
            u_s, db, t_s, dseq, t_s, t_s, samp_big, acc_s,
            (state_mlstm_n[l], state_mlstm_m[l], state_conv[l], state_shift[l]))

        unpad = lambda y: y.reshape(db, t_s, MIX)[:, :dseq].reshape(ms, MIX).astype(bf16)
        ya = jnp.concatenate([ya_p, unpad(ya_s)], axis=0)
        yb = jnp.concatenate([yb_p, unpad(yb_s)], axis=0)
        yc = jnp.concatenate([yc_p, unpad(yc_s)], axis=0)

        last = lambda k: jnp.stack([u[b * seq + seq - k:(b + 1) * seq] for b in range(nb)])
        us3 = u[mp:].reshape(db, dseq, P_PAD)
        conv_rows = lambda z: z[:, -(CONV_W - 1):, C_XS:C_XS + CONV_DIM]
        shift_row = lambda z: jnp.concatenate([z[:, -1, C_R:C_R + 3 * MIX],
                                               z[:, -1, C_LORA:C_LORA + (R_W - 3 * MIX)]], axis=-1)
        zp = last(CONV_W - 1)
        new_p[0].append(big_p[0][0]); new_p[3].append(big_p[1][0]); new_p[5].append(big_p[2][0])
        for lst, small, z3 in ((new_p, small_p, zp), (new_s, small_s, us3)):
            lst[1].append(small[0]); lst[2].append(small[1])
            lst[4].append(conv_rows(z3)); lst[6].append(shift_row(z3))

        mixed = _branch_mix(ya, yb, yc, u, w_branch_a[l].astype(bf16), w_branch_b[l].astype(bf16),
                            w_branch_c[l].astype(bf16))
        x1, xn2 = _outproj(x, mixed, w_out[l].astype(bf16), norm_ffn[l])
        j = l // 2
        if l % 2 == 0:
            h = _ffn_up(xn2, ffn_w_gate[j], ffn_w_up[j])
            x = _ffn_down(h, ffn_w_down[j], x1)
        else:
            x = _moe(x1, xn2, norm_ffn[l], moe_router[j], moe_w_gate[j], moe_w_up[j], moe_w_down[j])
        if l + 1 < depth:
            xn = _rmsnorm(x, norm_mix[l + 1], bf16)

    y = _rmsnorm(x, norm_final, f32)
    y_prompt = y[:mp].reshape(nb, seq, d)
    y_sample = y[mp:].reshape(db, dseq, d)
    refs = (state_mlstm_C, state_mlstm_n, state_mlstm_m, state_ssm, state_conv, state_wkv, state_shift)
    outs_p = [jnp.stack(lst).astype(r.dtype) for lst, r in zip(new_p, refs)]
    new_s[0], new_s[3], new_s[5] = acc_s
    outs_s = [(v if i in (0, 3, 5) else jnp.stack(v)).astype(r.dtype) for i, (v, r) in enumerate(zip(new_s, refs))]
    return (y_prompt, y_sample, *outs_p, *outs_s)
```

```python
import functools

import jax
import jax.numpy as jnp
from jax import lax
from jax.experimental import pallas as pl
from jax.experimental.pallas import tpu as pltpu

f32 = jnp.float32
bf16 = jnp.bfloat16

D_MODEL = 2048
MIX = D_MODEL // 2
H_A, DK_A, DV_A = 4, 128, 256
GATE_CAP = 15.0
H_B, P_B, N_B, G_B = 16, 64, 128, 2
CONV_W = 4
CONV_DIM = MIX + 2 * G_B * N_B
H_C, K_C = 16, 64
LORA_W, LORA_A, LORA_G = 96, 96, 256
R_W = 3 * MIX + LORA_W + LORA_A + LORA_G
A_COLS = 2 * H_A * DK_A + 2 * MIX + 2 * H_A
B_COLS = MIX + CONV_DIM + H_B
D_FF = 5632
N_EXP = 8
D_FF_E = 2816
EPS = 1e-6
LN_EPS_C = 64e-5
NEG_BIG = -1e30

C_Q, C_K, C_V, C_O, C_Z, C_UG = 0, 512, 1024, 2048, 3072, 4096
C_R, C_KC, C_VC, C_XS, C_BC, C_LORA, C_GATE = 10240, 11264, 12288, 13312, 14336, 14848, 15360
P_PAD = 16384
LANES = 128
GL_I, GL_F, GL_DT = 0, H_A, 2 * H_A

VMEM_LIMIT = 56 * 1024 * 1024


def _cp(sem):
    return pltpu.CompilerParams(dimension_semantics=sem, vmem_limit_bytes=VMEM_LIMIT)


def _bf(x):
    return x.astype(bf16)


def _dot(a, b):
    return jnp.dot(_bf(a), _bf(b), preferred_element_type=f32)


def _dot_nt(a, b):
    return lax.dot_general(_bf(a), _bf(b), (((1,), (1,)), ((), ())), preferred_element_type=f32)


def _dot_tn(a, b):
    return lax.dot_general(_bf(a), _bf(b), (((0,), (0,)), ((), ())), preferred_element_type=f32)


def _split3(x):
    hi = _bf(x)
    r1 = x - hi.astype(f32)
    mid = _bf(r1)
    lo = _bf(r1 - mid.astype(f32))
    return hi, mid, lo


def _dot3(a01, x):
    hi, mid, lo = _split3(x)
    d = lambda p: jnp.dot(a01, p, preferred_element_type=f32)
    return d(hi) + d(mid) + d(lo)


def _dot3_tn(x, b01):
    hi, mid, lo = _split3(x)
    d = lambda p: lax.dot_general(p, b01, (((0,), (0,)), ((), ())), preferred_element_type=f32)
    return d(hi) + d(mid) + d(lo)


def _dotx3(a, b):
    a_hi = _bf(a)
    a_lo = _bf(a - a_hi.astype(f32))
    b_hi = _bf(b)
    b_lo = _bf(b - b_hi.astype(f32))
    d = lambda p, q: jnp.dot(p, q, preferred_element_type=f32)
    return d(a_hi, b_hi) + d(a_hi, b_lo) + d(a_lo, b_hi)


def _sigmoid(x):
    return 1.0 / (1.0 + jnp.exp(-x))


def _silu(x):
    return x * _sigmoid(x)


def _softplus(x):
    return jnp.maximum(x, 0.0) + jnp.log1p(jnp.exp(-jnp.abs(x)))


def _iota2(shape, axis):
    return lax.broadcasted_iota(jnp.int32, shape, axis)


def _tri_masks(L):
    r = _iota2((L, L), 0)
    c = _iota2((L, L), 1)
    tril_b = c <= r
    tril = jnp.where(tril_b, 1.0, 0.0).astype(bf16)
    triu = jnp.where(r <= c, 1.0, 0.0).astype(bf16)
    eye = jnp.where(r == c, 1.0, 0.0).astype(bf16)
    return tril_b, tril, triu, eye


def _rmsnorm_body(x_ref, g_ref, o_ref):
    x = x_ref[...]
    y = x * lax.rsqrt(jnp.mean(x * x, axis=-1, keepdims=True) + EPS) * g_ref[...]
    o_ref[...] = y.astype(o_ref.dtype)


def _rmsnorm(x, g, out_dtype, tm=512):
    m, d = x.shape
    return pl.pallas_call(
        _rmsnorm_body,
        grid=(m // tm,),
        in_specs=[pl.BlockSpec((tm, d), lambda i: (i, 0)), pl.BlockSpec((1, d), lambda i: (0, 0))],
        out_specs=pl.BlockSpec((tm, d), lambda i: (i, 0)),
        out_shape=jax.ShapeDtypeStruct((m, d), out_dtype),
        compiler_params=_cp(("parallel",)),
        name="rmsnorm",
    )(x, g.reshape(1, d))


def _proj_in_body(x_ref, w_ref, o_ref):
    o_ref[...] = jnp.dot(x_ref[...], w_ref[...], preferred_element_type=f32)


def _proj_in(xn, w, tm=1088, tn=1024):
    m, k = xn.shape
    n = w.shape[1]
    return pl.pallas_call(
        _proj_in_body,
        grid=(n // tn, m // tm),
        in_specs=[pl.BlockSpec((tm, k), lambda j, i: (i, 0)), pl.BlockSpec((k, tn), lambda j, i: (0, j))],
        out_specs=pl.BlockSpec((tm, tn), lambda j, i: (i, j)),
        out_shape=jax.ShapeDtypeStruct((m, n), f32),
        compiler_params=_cp(("parallel", "parallel")),
        name="proj_in",
    )(xn, w)


def _branch_body(ya_ref, yb_ref, yc_ref, ga_ref, gb_ref, gc_ref, wa_ref, wb_ref, wc_ref, o_ref):
    d = lambda y, w: jnp.dot(y[...], w[...], preferred_element_type=f32)
    acc = (_sigmoid(ga_ref[...]) * d(ya_ref, wa_ref) + _sigmoid(gb_ref[...]) * d(yb_ref, wb_ref)
           + _sigmoid(gc_ref[...]) * d(yc_ref, wc_ref))
    o_ref[...] = acc.astype(o_ref.dtype)


def _branch_mix(ya, yb, yc, u, wa, wb, wc, tm=512, tn=512):
    m = ya.shape[0]
    gb0 = C_UG // tn
    nb = D_MODEL // tn
    yspec = pl.BlockSpec((tm, MIX), lambda j, i: (i, 0))
    wspec = pl.BlockSpec((MIX, tn), lambda j, i: (0, j))
    gspec = lambda off: pl.BlockSpec((tm, tn), lambda j, i: (i, gb0 + off * nb + j))
    return pl.pallas_call(
        _branch_body,
        grid=(nb, m // tm),
        in_specs=[yspec, yspec, yspec, gspec(0), gspec(1), gspec(2), wspec, wspec, wspec],
        out_specs=pl.BlockSpec((tm, tn), lambda j, i: (i, j)),
        out_shape=jax.ShapeDtypeStruct((m, D_MODEL), bf16),
        compiler_params=_cp(("parallel", "parallel")),
        name="branch_mix",
    )(ya, yb, yc, u, u, u, wa, wb, wc)


def _outproj_body(x_ref, mx_ref, w_ref, g_ref, x1_ref, xn_ref):
    x1 = x_ref[...] + jnp.dot(mx_ref[...], w_ref[...], preferred_element_type=f32)
    x1_ref[...] = x1
    xn = x1 * lax.rsqrt(jnp.mean(x1 * x1, axis=-1, keepdims=True) + EPS) * g_ref[...]
    xn_ref[...] = xn.astype(xn_ref.dtype)


def _outproj(x, mixed, w, g, tm=256):
    m, d = x.shape
    row = pl.BlockSpec((tm, d), lambda i: (i, 0))
    return pl.pallas_call(
        _outproj_body,
        grid=(m // tm,),
        in_specs=[row, row, pl.BlockSpec((d, d), lambda i: (0, 0)), pl.BlockSpec((1, d), lambda i: (0, 0))],
        out_specs=[row, row],
        out_shape=[jax.ShapeDtypeStruct((m, d), f32), jax.ShapeDtypeStruct((m, d), bf16)],
        compiler_params=_cp(("parallel",)),
        name="outproj",
    )(x, mixed, w, g.reshape(1, d))


def _ffn_up_body(x_ref, wg_ref, wu_ref, o_ref, wgb, wub):
    @pl.when(pl.program_id(1) == 0)
    def _():
        wgb[...] = _bf(wg_ref[...])
        wub[...] = _bf(wu_ref[...])

    x = x_ref[...]
    a = jnp.dot(x, wgb[...], preferred_element_type=f32)
    b = jnp.dot(x, wub[...], preferred_element_type=f32)
    o_ref[...] = (_silu(a) * b).astype(o_ref.dtype)


def _ffn_up(xn, wg, wu, tm=512, tn=512):
    m, k = xn.shape
    n = wg.shape[1]
    wspec = pl.BlockSpec((k, tn), lambda j, i: (0, j))
    return pl.pallas_call(
        _ffn_up_body,
        grid=(n // tn, m // tm),
        in_specs=[pl.BlockSpec((tm, k), lambda j, i: (i, 0)), wspec, wspec],
        out_specs=pl.BlockSpec((tm, tn), lambda j, i: (i, j)),
        out_shape=jax.ShapeDtypeStruct((m, n), bf16),
        scratch_shapes=[pltpu.VMEM((k, tn), bf16), pltpu.VMEM((k, tn), bf16)],
        compiler_params=_cp(("parallel", "arbitrary")),
        name="ffn_up",
    )(xn, wg, wu)


def _ffn_down_body(h_ref, w_ref, x_ref, o_ref, wb):
    @pl.when(pl.program_id(1) == 0)
    def _():
        wb[...] = _bf(w_ref[...])

    o_ref[...] = x_ref[...] + jnp.dot(h_ref[...], wb[...], preferred_element_type=f32)


def _ffn_down(h, w, x, tm=256, tn=512):
    m, k = h.shape
    n = w.shape[1]
    return pl.pallas_call(
        _ffn_down_body,
        grid=(n // tn, m // tm),
        in_specs=[pl.BlockSpec((tm, k), lambda j, i: (i, 0)), pl.BlockSpec((k, tn), lambda j, i: (0, j)),
                  pl.BlockSpec((tm, tn), lambda j, i: (i, j))],
        out_specs=pl.BlockSpec((tm, tn), lambda j, i: (i, j)),
        out_shape=jax.ShapeDtypeStruct((m, n), f32),
        scratch_shapes=[pltpu.VMEM((k, tn), bf16)],
        compiler_params=_cp(("parallel", "arbitrary")),
        name="ffn_down",
    )(h, w, x)


def _router_body(x_ref, g_ref, r_ref, ti_ref, tw_ref):
    x = x_ref[...]
    xn = x * lax.rsqrt(jnp.mean(x * x, axis=-1, keepdims=True) + EPS) * g_ref[...]
    logits = jnp.dot(xn, r_ref[...], preferred_element_type=f32, precision=lax.Precision.HIGHEST)
    lane_i = _iota2(logits.shape, 1)
    lane = lane_i.astype(f32)
    lg = jnp.where(lane_i < N_EXP, logits, -jnp.inf)
    v1 = jnp.max(lg, axis=1, keepdims=True)
    i1 = jnp.min(jnp.where(lg == v1, lane, float(LANES)), axis=1, keepdims=True)
    lg2 = jnp.where(lane == i1, -jnp.inf, lg)
    v2 = jnp.max(lg2, axis=1, keepdims=True)
    i2 = jnp.min(jnp.where(lg2 == v2, lane, float(LANES)), axis=1, keepdims=True)
    e2 = jnp.exp(v2 - v1)
    w1 = 1.0 / (1.0 + e2)
    w2 = e2 / (1.0 + e2)
    ti_ref[...] = jnp.where(lane_i == 0, i1, jnp.where(lane_i == 1, i2, 0.0)).astype(jnp.int32)
    tw_ref[...] = jnp.where(lane_i == 0, w1, jnp.where(lane_i == 1, w2, 0.0))


def _router(x, g, router, tm=512):
    m, d = x.shape
    rp = jnp.zeros((d, LANES), f32).at[:, :N_EXP].set(router)
    row = pl.BlockSpec((tm, LANES), lambda i: (i, 0))
    return pl.pallas_call(
        _router_body,
        grid=(m // tm,),
        in_specs=[pl.BlockSpec((tm, d), lambda i: (i, 0)), pl.BlockSpec((1, d), lambda i: (0, 0)),
                  pl.BlockSpec((d, LANES), lambda i: (0, 0))],
        out_specs=[row, row],
        out_shape=[jax.ShapeDtypeStruct((m, LANES), jnp.int32), jax.ShapeDtypeStruct((m, LANES), f32)],
        compiler_params=_cp(("parallel",)),
        name="router",
    )(x, g.reshape(1, d), rp)


def _new_expert(te_ref, i):
    prev = te_ref[jnp.maximum(i - 1, 0)]
    return jnp.logical_or(i == 0, te_ref[i] != prev)


MOE_SUB = 256


def _for_valid_rows(tv_ref, i, tm, compute, o_ref):
    for sb in range(tm // MOE_SUB):
        rows = pl.ds(sb * MOE_SUB, MOE_SUB)

        @pl.when(tv_ref[i] > sb * MOE_SUB)
        def _():
            compute(rows)

        @pl.when(tv_ref[i] <= sb * MOE_SUB)
        def _():
            o_ref[rows, :] = jnp.zeros((MOE_SUB, o_ref.shape[1]), o_ref.dtype)


def _moe_up_body(te_ref, tv_ref, x_ref, wg_ref, wu_ref, o_ref, wgb, wub):
    i = pl.program_id(1)

    @pl.when(_new_expert(te_ref, i))
    def _():
        wgb[...] = _bf(wg_ref[0])
        wub[...] = _bf(wu_ref[0])

    def compute(rows):
        x = x_ref[rows, :]
        a = jnp.dot(x, wgb[...], preferred_element_type=f32)
        b = jnp.dot(x, wub[...], preferred_element_type=f32)
        o_ref[rows, :] = (_silu(a) * b).astype(o_ref.dtype)

    _for_valid_rows(tv_ref, i, x_ref.shape[0], compute, o_ref)


def _moe_up(xg, wg, wu, te, tv, tm, tn=256):
    p, k = xg.shape
    n = wg.shape[2]
    wspec = pl.BlockSpec((1, k, tn), lambda j, i, te, tv: (te[i], 0, j))
    return pl.pallas_call(
        _moe_up_body,
        grid_spec=pltpu.PrefetchScalarGridSpec(
            num_scalar_prefetch=2,
            grid=(n // tn, p // tm),
            in_specs=[pl.BlockSpec((tm, k), lambda j, i, te, tv: (i, 0)), wspec, wspec],
            out_specs=pl.BlockSpec((tm, tn), lambda j, i, te, tv: (i, j)),
            scratch_shapes=[pltpu.VMEM((k, tn), bf16), pltpu.VMEM((k, tn), bf16)],
        ),
        out_shape=jax.ShapeDtypeStruct((p, n), bf16),
        compiler_params=_cp(("parallel", "arbitrary")),
        name="moe_up",
    )(te, tv, xg, wg, wu)


def _moe_down_body(te_ref, tv_ref, h_ref, w_ref, rw_ref, o_ref, wb):
    i = pl.program_id(1)

    @pl.when(_new_expert(te_ref, i))
    def _():
        wb[...] = _bf(w_ref[0])

    def compute(rows):
        o_ref[rows, :] = jnp.dot(h_ref[rows, :], wb[...], preferred_element_type=f32) * rw_ref[rows, 0:1]

    _for_valid_rows(tv_ref, i, h_ref.shape[0], compute, o_ref)


def _moe_down(hg, wd, roww, te, tv, tm, tn=512):
    p, k = hg.shape
    n = wd.shape[2]
    return pl.pallas_call(
        _moe_down_body,
        grid_spec=pltpu.PrefetchScalarGridSpec(
            num_scalar_prefetch=2,
            grid=(n // tn, p // tm),
            in_specs=[pl.BlockSpec((tm, k), lambda j, i, te, tv: (i, 0)),
                      pl.BlockSpec((1, k, tn), lambda j, i, te, tv: (te[i], 0, j)),
                      pl.BlockSpec((tm, LANES), lambda j, i, te, tv: (i, 0))],
            out_specs=pl.BlockSpec((tm, tn), lambda j, i, te, tv: (i, j)),
            scratch_shapes=[pltpu.VMEM((k, tn), bf16)],
        ),
        out_shape=jax.ShapeDtypeStruct((p, n), f32),
        compiler_params=_cp(("parallel", "arbitrary")),
        name="moe_down",
    )(te, tv, hg, wd, roww)


def _moe(x1, xn_bf, g, router, wg, wu, wd, tm=512):
    m = x1.shape[0]
    top_i, top_w = _router(x1, g, router)
    e_flat = top_i[:, :2].reshape(-1)
    w_flat = top_w[:, :2].reshape(-1)
    npair = 2 * m
    ntile = npair // tm + N_EXP
    ptot = ntile * tm
    experts = jnp.arange(N_EXP, dtype=jnp.int32)
    counts = jnp.sum((e_flat[:, None] == experts[None, :]).astype(jnp.int32), axis=0)
    tiles_per = (counts + tm - 1) // tm
    tile_end = jnp.cumsum(tiles_per)
    fill_end = jnp.cumsum(tiles_per * tm - counts)
    fidx = jnp.arange(ptot - npair, dtype=jnp.int32)
    fill_key = jnp.sum((fidx[:, None] >= fill_end[None, :]).astype(jnp.int32), axis=1)
    keys = jnp.concatenate([e_flat, fill_key])
    ids = jnp.arange(ptot, dtype=jnp.int32)
    wts = jnp.concatenate([w_flat, jnp.zeros((ptot - npair,), f32)])
    _, id_s, roww = lax.sort((keys, ids, wts), num_keys=1, is_stable=True)
    src_tok = jnp.where(id_s < npair, id_s >> 1, 0)
    _, slot_of = lax.sort((id_s, ids), num_keys=1)
    pos = slot_of[:npair].reshape(m, 2)
    tidx = jnp.arange(ntile, dtype=jnp.int32)
    te = jnp.minimum(jnp.sum((tidx[:, None] >= tile_end[None, :]).astype(jnp.int32), axis=1), N_EXP - 1)
    first_tile = (tile_end - tiles_per)[te]
    tv = jnp.clip(counts[te] - (tidx - first_tile) * tm, 0, tm)
    tv = jnp.where(tidx < tile_end[-1], tv, 0).astype(jnp.int32)
    te = jnp.where(tv > 0, te, te[jnp.maximum(tile_end[-1] - 1, 0)]).astype(jnp.int32)

    d = xn_bf.shape[1]
    xg = jnp.take(xn_bf.reshape(m, d // LANES, LANES), src_tok, axis=0, mode="clip").reshape(ptot, d)
    hg = _moe_up(xg, wg, wu, te, tv, tm)
    yg = _moe_down(hg, wd, jnp.broadcast_to(roww[:, None], (ptot, LANES)), te, tv, tm)
    return x1 + jnp.take(yg, pos[:, 0], axis=0, mode="clip") + jnp.take(yg, pos[:, 1], axis=0, mode="clip")


def _block_masks(R, lsub):
    lg = lsub.bit_length() - 1
    r = _iota2((R, R), 0)
    c = _iota2((R, R), 1)
    same = (r >> lg) == (c >> lg)
    tril_b = jnp.logical_and(same, c <= r)
    one = lambda m: jnp.where(m, 1.0, 0.0).astype(bf16)
    return tril_b, one(tril_b), one(jnp.logical_and(same, r <= c)), one(r == c)


def _per_seq_col(vals, lsub):
    return jnp.concatenate([jnp.broadcast_to(v, (lsub, 1)) for v in vals], axis=0)


def _mlstm_body(*refs, R, nslot, t_valid, n_in):
    q_ref, k_ref, v_ref, o_ref, g_ref, bias_ref, ng_ref, c0_ref, n0_ref, m0_ref = refs[:10]
    y_ref, c_ref, n_ref, m_ref = refs[n_in:n_in + 4]
    c = pl.program_id(1)
    lsub = R // nslot
    slots = range(nslot)
    heads = range(H_A)
    rows = [slice(s * lsub, (s + 1) * lsub) for s in slots]

    @pl.when(c == 0)
    def _():
        c_ref[...] = c0_ref[...]
        n_ref[...] = n0_ref[...]
        m_ref[...] = m0_ref[...]

    tril_b, tril, triu, eye = _block_masks(R, lsub)
    tg = GATE_CAP * jnp.tanh((g_ref[...] + bias_ref[...]) / GATE_CAP)
    li_all = tg
    lf_all = jnp.minimum(tg, 0.0) - jnp.log1p(jnp.exp(-jnp.abs(tg)))
    if t_valid is not None:
        valid = (_iota2((R, LANES), 0) & (lsub - 1)) < t_valid
        li_all = jnp.where(valid, li_all, NEG_BIG)
        lf_all = jnp.where(valid, lf_all, 0.0)
    b_all = _dot3(tril, lf_all)
    b_rows = _dot3_tn(lf_all, triu)
    li_rows = _dot3_tn(li_all, eye)

    q = [q_ref[:, h * DK_A:(h + 1) * DK_A] for h in heads]
    k = [k_ref[:, h * DK_A:(h + 1) * DK_A] * (DK_A ** -0.5) for h in heads]
    v = [v_ref[:, h * DV_A:(h + 1) * DV_A] for h in heads]
    qb, kb, vb = [_bf(z) for z in q], [_bf(z) for z in k], [_bf(z) for z in v]
    bc = [b_all[:, GL_F + h:GL_F + h + 1] for h in heads]
    lc = [li_all[:, GL_I + h:GL_I + h + 1] for h in heads]
    m_prev = [[m_ref[s, h:h + 1, 0:1] for s in slots] for h in heads]
    c_prev = [[c_ref[s, h] for s in slots] for h in heads]
    n_prev = [[n_ref[s, h:h + 1, :] for s in slots] for h in heads]

    qk = [_dot_nt(qb[h], kb[h]) for h in heads]
    inter = [jnp.concatenate([_dot(q[h][rows[s]], c_prev[h][s]) for s in slots], axis=0) for h in heads]
    hh = []
    for h in heads:
        br = b_rows[GL_F + h:GL_F + h + 1, :]
        lr = li_rows[GL_I + h:GL_I + h + 1, :]
        dmat = jnp.where(tril_b, bc[h] - br + lr, -jnp.inf)
        g_inter = bc[h] + _per_seq_col(m_prev[h], lsub)
        m_t = jnp.maximum(g_inter, jnp.max(dmat, axis=1, keepdims=True))
        w_inter = jnp.exp(g_inter - m_t)
        s_mat = qk[h] * jnp.exp(dmat - m_t)
        n_rows = jnp.concatenate([jnp.broadcast_to(n_prev[h][s], (lsub, DK_A)) for s in slots], axis=0)
        num = w_inter * inter[h] + _dot(s_mat, vb[h])
        den = w_inter * jnp.sum(q[h] * n_rows, axis=1, keepdims=True) + jnp.sum(s_mat, axis=1, keepdims=True)
        hh.append(num / jnp.maximum(jnp.abs(den), jnp.exp(-m_t)))

    for h in heads:
        for s in slots:
            bcs, mp = bc[h][rows[s]], m_prev[h][s]
            b_last = bcs[lsub - 1:lsub, :]
            g_s = b_last - bcs + lc[h][rows[s]]
            m_new = jnp.maximum(b_last + mp, jnp.max(g_s, axis=0, keepdims=True))
            decay = jnp.exp(b_last + mp - m_new)
            ak = jnp.exp(g_s - m_new) * k[h][rows[s]]
            c_ref[s, h] = decay * c_prev[h][s] + _dot_tn(ak, v[h][rows[s]])
            n_ref[s, h:h + 1, :] = decay * n_prev[h][s] + jnp.sum(ak, axis=0, keepdims=True)
            m_ref[s, h:h + 1, :] = jnp.broadcast_to(m_new, (1, LANES))

    for h in heads:
        cols = slice(h * DV_A, (h + 1) * DV_A)
        hn = hh[h] * lax.rsqrt(jnp.mean(hh[h] * hh[h], axis=-1, keepdims=True) + EPS) * ng_ref[:, cols]
        y_ref[:, cols] = (hn * _sigmoid(o_ref[:, cols])).astype(y_ref.dtype)


def _state_io(tail, st, acc, nslot):
    zeros = (0,) * len(tail)
    in_spec = pl.BlockSpec((None, nslot) + tail, lambda b, c: (st.li, b) + zeros)
    out_spec = pl.BlockSpec((None, nslot) + tail, lambda b, c: (st.lo, b) + zeros)
    extra_in = [] if acc is None else [pl.BlockSpec(memory_space=pl.ANY)]
    extra_args = [] if acc is None else [acc]
    return in_spec, out_spec, extra_in, extra_args


def _tiling(bt, t, rows):
    nslot = 1 if t >= rows else rows // t
    return nslot, max(t // rows, 1), bt // nslot


class _St:
    def __init__(self, arr, li, lo, depth):
        self.arr, self.li, self.lo, self.depth = arr, li, lo, depth


def _mlstm(u, bias_row, norm_g, st, acc, n0, m0, *, bt, t, rows, t_valid):
    nslot, nc, nblk = _tiling(bt, t, rows)
    R = rows
    rb = lambda b, c: b * nc + c
    m0b = jnp.broadcast_to(m0[:, :, None], (bt, H_A, LANES))
    c_in, c_out, extra_in, extra_args = _state_io((H_A, DK_A, DV_A), st, acc, nslot)
    in_specs = [
        pl.BlockSpec((R, 512), lambda b, c: (rb(b, c), C_Q // 512)),
        pl.BlockSpec((R, 512), lambda b, c: (rb(b, c), C_K // 512)),
        pl.BlockSpec((R, MIX), lambda b, c: (rb(b, c), C_V // MIX)),
        pl.BlockSpec((R, MIX), lambda b, c: (rb(b, c), C_O // MIX)),
        pl.BlockSpec((R, LANES), lambda b, c: (rb(b, c), C_GATE // LANES)),
        pl.BlockSpec((1, LANES), lambda b, c: (0, 0)),
        pl.BlockSpec((1, MIX), lambda b, c: (0, 0)),
        c_in,
        pl.BlockSpec((nslot, H_A, DK_A), lambda b, c: (b, 0, 0)),
        pl.BlockSpec((nslot, H_A, LANES), lambda b, c: (b, 0, 0)),
    ] + extra_in
    out_specs = [
        pl.BlockSpec((R, MIX), lambda b, c: (rb(b, c), 0)),
        c_out,
        pl.BlockSpec((nslot, H_A, DK_A), lambda b, c: (b, 0, 0)),
        pl.BlockSpec((nslot, H_A, LANES), lambda b, c: (b, 0, 0)),
    ]
    out_shape = [
        jax.ShapeDtypeStruct((bt * t, MIX), bf16),
        jax.ShapeDtypeStruct((st.depth, bt, H_A, DK_A, DV_A), f32),
        jax.ShapeDtypeStruct((bt, H_A, DK_A), f32),
        jax.ShapeDtypeStruct((bt, H_A, LANES), f32),
    ]
    y, c_new, n_new, m_new = pl.pallas_call(
        functools.partial(_mlstm_body, R=R, nslot=nslot, t_valid=t_valid, n_in=len(in_specs)),
        grid=(nblk, nc), in_specs=in_specs, out_specs=out_specs, out_shape=out_shape,
        input_output_aliases={} if acc is None else {len(in_specs) - 1: 1},
        compiler_params=_cp(("parallel", "arbitrary")), name="mlstm",
    )(u, u, u, u, u, bias_row, norm_g.reshape(1, MIX), st.arr, n0, m0b, *extra_args)
    return y, c_new, n_new, m_new[:, :, 0]


def _ssd_body(*refs, R, nslot, t_valid, n_in):
    (z_ref, xs_ref, bcr_ref, g_ref, bias_ref, cwx_ref, cwb_ref, cbx_ref, cbb_ref, alog_ref,
     d_ref, ng_ref, cx0_ref, cb0_ref, h0_ref) = refs[:15]
    y_ref, h_ref, px_sc, pb_sc = refs[n_in:n_in + 4]
    c = pl.program_id(1)
    lsub = R // nslot
    slots = range(nslot)
    rows = [slice(s * lsub, (s + 1) * lsub) for s in slots]

    @pl.when(c == 0)
    def _():
        h_ref[...] = h0_ref[...]
        if nslot == 1:
            px_sc[...] = jnp.zeros(px_sc.shape, f32)
            pb_sc[...] = jnp.zeros(pb_sc.shape, f32)
            px_sc[R - 8:R, :] = cx0_ref[...]
            pb_sc[R - 8:R, :] = cb0_ref[...]

    def conv(cur, prev, cw_ref, cb_ref):
        tpos = _iota2(cur.shape, 0) & (lsub - 1)
        acc = cb_ref[...] + cw_ref[CONV_W - 1:CONV_W, :] * cur
        for sft in range(1, CONV_W):
            shifted = jnp.where(tpos >= sft, pltpu.roll(cur, sft, 0), pltpu.roll(prev, sft, 0))
            acc = acc + cw_ref[CONV_W - 1 - sft:CONV_W - sft, :] * shifted
        return _silu(acc)

    cur_x = xs_ref[...]
    cur_b = bcr_ref[...]
    prev_x = px_sc[...] if nslot == 1 else cx0_ref[...]
    prev_b = pb_sc[...] if nslot == 1 else cb0_ref[...]
    xs = conv(cur_x, prev_x, cwx_ref, cbx_ref)
    bcm = conv(cur_b, prev_b, cwb_ref, cbb_ref)
    if nslot == 1:
        px_sc[...] = cur_x
        pb_sc[...] = cur_b

    tril_b, tril, triu, eye = _block_masks(R, lsub)
    dt_all = _softplus(g_ref[...] + bias_ref[...])
    if t_valid is not None:
        valid = (_iota2((R, LANES), 0) & (lsub - 1)) < t_valid
        dt_all = jnp.where(valid, dt_all, 0.0)
    da_all = dt_all * (-jnp.exp(alog_ref[...]))
    acs_all = _dot3(tril, da_all)
    acs_rows = _dot3_tn(da_all, triu)
    dt_rows = _dot3_tn(dt_all, eye)

    lane0 = _iota2((R, LANES), 1) < P_B
    row0 = _iota2((2 * P_B, 1), 0) < P_B
    npair = H_B // 2
    bm = [bcm[:, g * N_B:(g + 1) * N_B] for g in range(G_B)]
    cm = [bcm[:, (G_B + g) * N_B:(G_B + g + 1) * N_B] for g in range(G_B)]
    cb = [_dot_nt(cm[g], bm[g]) for g in range(G_B)]

    def pair_group(pairs):
        grp = {p: (2 * p) // (H_B // G_B) for p in pairs}
        xp = {p: xs[:, p * LANES:(p + 1) * LANES] for p in pairs}
        hp = {p: [h_ref[s, 2 * p:2 * p + 2].reshape(2 * P_B, N_B) for s in slots] for p in pairs}
        wts, acs, tails, alast = {}, {}, {}, {}
        for hd in [2 * p + i for p in pairs for i in (0, 1)]:
            ln = GL_DT + hd
            ac = acs_all[:, ln:ln + 1]
            seg = jnp.where(tril_b, ac - acs_rows[ln:ln + 1, :], -jnp.inf)
            wts[hd] = _bf(cb[grp[hd // 2]] * jnp.exp(seg) * dt_rows[ln:ln + 1, :])
            al = [ac[(s + 1) * lsub - 1:(s + 1) * lsub, :] for s in slots]
            acs[hd], alast[hd] = ac, al
            tails[hd] = jnp.exp(_per_seq_col(al, lsub) - ac) * dt_all[:, ln:ln + 1]
        xpb = {p: _bf(xp[p]) for p in pairs}
        intra = {p: jnp.where(lane0, jnp.dot(wts[2 * p], xpb[p], preferred_element_type=f32),
                              jnp.dot(wts[2 * p + 1], xpb[p], preferred_element_type=f32)) for p in pairs}
        inter = {p: jnp.concatenate([_dot_nt(cm[grp[p]][rows[s]], hp[p][s]) for s in slots], axis=0) for p in pairs}
        for p in pairs:
            tailx = xp[p] * jnp.where(lane0, tails[2 * p], tails[2 * p + 1])
            for s in slots:
                dec = jnp.where(row0, jnp.exp(alast[2 * p][s]), jnp.exp(alast[2 * p + 1][s]))
                new = dec * hp[p][s] + _dot_tn(tailx[rows[s]], bm[grp[p]][rows[s]])
                h_ref[s, 2 * p:2 * p + 2] = new.reshape(2, P_B, N_B)
        return [intra[p] + inter[p] * jnp.where(lane0, jnp.exp(acs[2 * p]), jnp.exp(acs[2 * p + 1]))
                + d_ref[:, p * LANES:(p + 1) * LANES] * xp[p] for p in pairs]

    gsz = 1 if nslot == 1 else npair
    ys = []
    for p0 in range(0, npair, gsz):
        ys += pair_group(list(range(p0, p0 + gsz)))

    y = jnp.concatenate(ys, axis=1) * _silu(z_ref[...])
    gw = MIX // G_B
    outs = []
    for grp in range(G_B):
        sg = y[:, grp * gw:(grp + 1) * gw]
        outs.append(sg * lax.rsqrt(jnp.mean(sg * sg, axis=-1, keepdims=True) + EPS))
    y_ref[...] = (jnp.concatenate(outs, axis=1) * ng_ref[...]).astype(y_ref.dtype)


def _ssd(u, bias_row, conv_w, conv_b, a_log, d_skip, norm_g, conv0, st, acc, *, bt, t, rows, t_valid):
    nslot, nc, nblk = _tiling(bt, t, rows)
    R = rows
    lsub = R // nslot
    rb = lambda b, c: b * nc + c
    const = lambda shape: pl.BlockSpec(shape, lambda b, c: tuple(0 for _ in shape))
    alog_row = jnp.zeros((1, LANES), f32).at[0, GL_DT:GL_DT + H_B].set(a_log)
    d_row = jnp.repeat(d_skip, P_B).reshape(1, MIX)
    crows = 8 if nslot == 1 else lsub
    if nslot > 1:
        conv0 = jnp.roll(conv0.reshape(nblk, nslot, CONV_W - 1, CONV_DIM), -1, axis=1).reshape(bt, CONV_W - 1, CONV_DIM)
    conv0p = jnp.pad(conv0, ((0, 0), (crows - (CONV_W - 1), 0), (0, 0))).reshape(bt * crows, CONV_DIM)
    cblk = 8 if nslot == 1 else R
    h_in, h_out, extra_in, extra_args = _state_io((H_B, P_B, N_B), st, acc, nslot)
    in_specs = [
        pl.BlockSpec((R, MIX), lambda b, c: (rb(b, c), C_Z // MIX)),
        pl.BlockSpec((R, MIX), lambda b, c: (rb(b, c), C_XS // MIX)),
        pl.BlockSpec((R, 512), lambda b, c: (rb(b, c), C_BC // 512)),
        pl.BlockSpec((R, LANES), lambda b, c: (rb(b, c), C_GATE // LANES)),
        const((1, LANES)), const((CONV_W, MIX)), const((CONV_W, 512)), const((1, MIX)), const((1, 512)),
        const((1, LANES)), const((1, MIX)), const((1, MIX)),
        pl.BlockSpec((cblk, MIX), lambda b, c: (b, 0)),
        pl.BlockSpec((cblk, 512), lambda b, c: (b, 0)),
        h_in,
    ] + extra_in
    out_specs = [pl.BlockSpec((R, MIX), lambda b, c: (rb(b, c), 0)), h_out]
    out_shape = [jax.ShapeDtypeStruct((bt * t, MIX), bf16), jax.ShapeDtypeStruct((st.depth, bt, H_B, P_B, N_B), f32)]
    return pl.pallas_call(
        functools.partial(_ssd_body, R=R, nslot=nslot, t_valid=t_valid, n_in=len(in_specs)),
        grid=(nblk, nc), in_specs=in_specs, out_specs=out_specs, out_shape=out_shape,
        input_output_aliases={} if acc is None else {len(in_specs) - 1: 1},
        scratch_shapes=[pltpu.VMEM((R, MIX), f32), pltpu.VMEM((R, 512), f32)],
        compiler_params=_cp(("parallel", "arbitrary")), name="ssd",
    )(u, u, u, u, bias_row, conv_w[:, :MIX], conv_w[:, MIX:], conv_b[:MIX].reshape(1, MIX),
      conv_b[MIX:].reshape(1, 512), alog_row, d_row, norm_g.reshape(1, MIX),
      conv0p[:, :MIX], conv0p[:, MIX:], st.arr, *extra_args)


PR_MU_R, PR_MU_K, PR_MU_V, PR_W0, PR_A0, PR_KK, PR_KA, PR_RK, PR_LNW, PR_LNB = range(10)


def _rwkv_body(*refs, R, nslot, t_valid, n_in):
    (r_ref, k_ref, v_ref, l_ref, par_ref, mul_ref, w2_ref, a2_ref, g2_ref,
     shr_ref, shk_ref, shv_ref, shl_ref, s0_ref) = refs[:14]
    y_ref, s_ref, cr_sc, ck_sc, cv_sc, cl_sc, st_sc = refs[n_in:n_in + 7]
    c = pl.program_id(1)
    nc = pl.num_programs(1)
    npair = H_C // 2
    lsub = R // nslot
    lg = lsub.bit_length() - 1
    L2 = 2 * R

    @pl.when(c == 0)
    def _():
        if nslot == 1:
            cr_sc[...] = shr_ref[...]
            ck_sc[...] = shk_ref[...]
            cv_sc[...] = shv_ref[...]
            cl_sc[...] = shl_ref[...]
        rowm = _iota2((2 * K_C, K_C), 0) < K_C
        for s in range(nslot):
            for p in range(npair):
                x = s0_ref[s, 2 * p:2 * p + 2].reshape(2 * K_C, K_C)
                st_sc[s * npair + p] = jnp.concatenate([jnp.where(rowm, x, 0.0), jnp.where(rowm, 0.0, x)], axis=1)

    par = lambda i: par_ref[i:i + 1, :]

    def shifted(cur_ref, carry, sh_ref, mu):
        cur = cur_ref[...]
        first = (_iota2(cur.shape, 0) & (lsub - 1)) == 0
        if nslot == 1:
            before = carry[0:1, :]
            carry[0:1, :] = cur[R - 1:R, :]
        else:
            before = sh_ref[...]
        prev = jnp.where(first, before, pltpu.roll(cur, 1, 0))
        return cur + (prev - cur) * mu

    xr = shifted(r_ref, cr_sc, shr_ref, par(PR_MU_R))
    xk = shifted(k_ref, ck_sc, shk_ref, par(PR_MU_K))
    xv = shifted(v_ref, cv_sc, shv_ref, par(PR_MU_V))
    xl = shifted(l_ref, cl_sc, shl_ref, mul_ref[...])

    lw = jnp.dot(_bf(jnp.tanh(xl)), w2_ref[...], preferred_element_type=f32)
    la = jnp.dot(_bf(xl), a2_ref[...], preferred_element_type=f32)
    g = jnp.dot(_bf(_sigmoid(xl)), g2_ref[...], preferred_element_type=f32)
    wlog = -_softplus(-(par(PR_W0) + lw)) - 0.5
    logdec = -jnp.exp(wlog)
    a = _sigmoid(par(PR_A0) + la)
    kkr = xk * par(PR_KK)
    k2 = xk * (1.0 + (a - 1.0) * par(PR_KA))
    if t_valid is not None:
        valid = (_iota2((R, MIX), 0) & (lsub - 1)) < t_valid
        logdec = jnp.where(valid, logdec, 0.0)
        kkr = jnp.where(valid, kkr, 0.0)
        k2 = jnp.where(valid, k2, 0.0)
        xv = jnp.where(valid, xv, 0.0)

    rr = _iota2((R, R), 0)
    cc = _iota2((R, R), 1)
    tril = jnp.where(jnp.logical_and(cc <= rr, (rr >> lg) == (cc >> lg)), 1.0, 0.0).astype(bf16)
    cum = _dot3(tril, logdec)
    pfull = jnp.exp(cum)
    pinv = jnp.exp(-cum)
    pprev = jnp.exp(cum - logdec)

    lane0 = _iota2((R, LANES), 1) < K_C
    r2 = _iota2((L2, L2), 0)
    c2 = _iota2((L2, L2), 1)
    same = (r2 >> lg) == (c2 >> lg)
    strict = jnp.logical_and(same, (c2 & (lsub - 1)) < (r2 & (lsub - 1)))
    incl = jnp.logical_and(same, (c2 & (lsub - 1)) <= (r2 & (lsub - 1)))
    eye2 = jnp.where(r2 == c2, 1.0, 0.0)

    def hsum(x):
        s0 = jnp.sum(jnp.where(lane0, x, 0.0), axis=1, keepdims=True)
        s1 = jnp.sum(jnp.where(lane0, 0.0, x), axis=1, keepdims=True)
        return jnp.where(lane0, s0, s1)

    def stack(x):
        h0 = jnp.where(lane0, x, 0.0)
        h1 = jnp.where(lane0, 0.0, x)
        pieces = []
        for s in range(nslot):
            pieces += [h0[s * lsub:(s + 1) * lsub], h1[s * lsub:(s + 1) * lsub]]
        return jnp.concatenate(pieces, axis=0)

    nsq = max(lg - 1, 0)
    w = 2 * lsub
    pairs = range(npair)
    sls = [slice(p * LANES, (p + 1) * LANES) for p in pairs]
    s_old = [[st_sc[s * npair + p] for s in range(nslot)] for p in pairs]
    kap2, bt2, kt2, rt2, v2 = [], [], [], [], []
    for p in pairs:
        sl = sls[p]
        kkp = kkr[:, sl]
        kap = kkp / jnp.maximum(jnp.sqrt(hsum(kkp * kkp)), 1e-12)
        kap2.append(_bf(stack(kap * pprev[:, sl])))
        bt2.append(stack(kap * a[:, sl] * pinv[:, sl]))
        kt2.append(stack(k2[:, sl] * pinv[:, sl]))
        rt2.append(_bf(stack(xr[:, sl] * pfull[:, sl])))
        v2.append(_bf(stack(xv[:, sl])))
    gram = [_dot_nt(jnp.concatenate([kap2[p], rt2[p]], axis=0),
                    jnp.concatenate([_bf(bt2[p]), _bf(kt2[p])], axis=0)) for p in pairs]
    n_kb = [jnp.where(strict, gram[p][0:L2, 0:L2], 0.0) for p in pairs]
    a_kr = [jnp.concatenate([jnp.where(strict, gram[p][0:L2, L2:2 * L2], 0.0),
                             jnp.where(incl, gram[p][L2:2 * L2, L2:2 * L2], 0.0)], axis=0) for p in pairs]
    a_rb = [jnp.where(incl, gram[p][L2:2 * L2, 0:L2], 0.0) for p in pairs]
    tinv = [eye2 - n_kb[p] for p in pairs]
    if nsq > 0:
        pw = [_dot(n_kb[p], n_kb[p]) for p in pairs]
        for it in range(nsq):
            if it + 1 < nsq:
                both = [_dot(jnp.concatenate([pw[p], tinv[p]], axis=0), pw[p]) for p in pairs]
                pw = [both[p][0:L2] for p in pairs]
                tinv = [tinv[p] + both[p][L2:2 * L2] for p in pairs]
            else:
                tinv = [tinv[p] + _dot(tinv[p], pw[p]) for p in pairs]
    ks = [[_dot_nt(jnp.concatenate([kap2[p][s * w:(s + 1) * w], rt2[p][s * w:(s + 1) * w]], axis=0), s_old[p][s])
           for s in range(nslot)] for p in pairs]
    av = [_dot(a_kr[p], v2[p]) for p in pairs]
    u2 = [_dot(tinv[p], jnp.concatenate([z[0:w] for z in ks[p]], axis=0) + av[p][0:L2]) for p in pairs]
    au = [_dot(a_rb[p], u2[p]) for p in pairs]
    for p in pairs:
        u2b = _bf(u2[p])
        for s in range(nslot):
            pl_row = pfull[(s + 1) * lsub - 1:(s + 1) * lsub, sls[p]]
            lhs = jnp.concatenate([v2[p][s * w:(s + 1) * w], -u2b[s * w:(s + 1) * w]], axis=0)
            rhs = jnp.concatenate([kt2[p][s * w:(s + 1) * w], bt2[p][s * w:(s + 1) * w]], axis=0) * pl_row
            st_sc[s * npair + p] = s_old[p][s] * pl_row + _dot_tn(lhs, rhs)
    for p in pairs:
        sl = sls[p]
        y2 = jnp.concatenate([z[w:2 * w] for z in ks[p]], axis=0) + av[p][L2:2 * L2] - au[p]
        yp = jnp.concatenate([y2[s * w:s * w + lsub] + y2[s * w + lsub:(s + 1) * w] for s in range(nslot)], axis=0)
        mean = hsum(yp) * (1.0 / K_C)
        dy = yp - mean
        var = hsum(dy * dy) * (1.0 / K_C)
        yn = dy * lax.rsqrt(var + LN_EPS_C) * par(PR_LNW)[:, sl] + par(PR_LNB)[:, sl]
        bonus = hsum(xr[:, sl] * k2[:, sl] * par(PR_RK)[:, sl]) * xv[:, sl]
        y_ref[:, sl] = ((yn + bonus) * g[:, sl]).astype(y_ref.dtype)

    @pl.when(c == nc - 1)
    def _():
        for s in range(nslot):
            for p in range(npair):
                blk = st_sc[s * npair + p]
                s_ref[s, 2 * p] = blk[0:K_C, 0:K_C]
                s_ref[s, 2 * p + 1] = blk[K_C:2 * K_C, K_C:2 * K_C]


RWKV_ROWS = 64
SAMPLE_ROWS = 64


def _rwkv(u, par, mu_l, w2p, a2p, g2p, shift0, st, acc, *, bt, t, t_valid):
    R = RWKV_ROWS
    nslot = 1 if t >= R else R // t
    nc = max(t // R, 1)
    nblk = bt // nslot
    rb = lambda b, c: b * nc + c
    const = lambda shape: pl.BlockSpec(shape, lambda b, c: tuple(0 for _ in shape))
    npair = H_C // 2
    lsub = R // nslot
    sh_rows = 8 if nslot == 1 else lsub
    padr = lambda x: jnp.pad(x[:, None, :], ((0, 0), (0, sh_rows - 1), (0, 0))).reshape(bt * sh_rows, x.shape[-1])
    sh_r = padr(shift0[:, 0:MIX])
    sh_k = padr(shift0[:, MIX:2 * MIX])
    sh_v = padr(shift0[:, 2 * MIX:3 * MIX])
    sh_l = padr(jnp.pad(shift0[:, 3 * MIX:], ((0, 0), (0, 512 - (R_W - 3 * MIX)))))
    shb = 8 if nslot == 1 else R
    tail = (H_C, K_C, K_C)
    zeros = (0,) * len(tail)
    s_in = pl.BlockSpec((None, nslot) + tail, lambda b, c: (st.li, b) + zeros)
    s_out = pl.BlockSpec((None, nslot) + tail, lambda b, c: (st.lo, b) + zeros)
    extra_in = [] if acc is None else [pl.BlockSpec(memory_space=pl.ANY)]
    extra_args = [] if acc is None else [acc]
    in_specs = [
        pl.BlockSpec((R, MIX), lambda b, c: (rb(b, c), C_R // MIX)),
        pl.BlockSpec((R, MIX), lambda b, c: (rb(b, c), C_KC // MIX)),
        pl.BlockSpec((R, MIX), lambda b, c: (rb(b, c), C_VC // MIX)),
        pl.BlockSpec((R, 512), lambda b, c: (rb(b, c), C_LORA // 512)),
        const((16, MIX)), const((1, 512)), const((512, MIX)), const((512, MIX)), const((512, MIX)),
        pl.BlockSpec((shb, MIX), lambda b, c: (b, 0)),
        pl.BlockSpec((shb, MIX), lambda b, c: (b, 0)),
        pl.BlockSpec((shb, MIX), lambda b, c: (b, 0)),
        pl.BlockSpec((shb, 512), lambda b, c: (b, 0)),
        s_in,
    ] + extra_in
    out_specs = [pl.BlockSpec((R, MIX), lambda b, c: (rb(b, c), 0)), s_out]
    out_shape = [jax.ShapeDtypeStruct((bt * t, MIX), bf16), jax.ShapeDtypeStruct((st.depth, bt) + tail, f32)]
    return pl.pallas_call(
        functools.partial(_rwkv_body, R=R, nslot=nslot, t_valid=t_valid, n_in=len(in_specs)),
        grid=(nblk, nc), in_specs=in_specs, out_specs=out_specs, out_shape=out_shape,
        input_output_aliases={} if acc is None else {len(in_specs) - 1: 1},
        scratch_shapes=[pltpu.VMEM((8, MIX), f32), pltpu.VMEM((8, MIX), f32), pltpu.VMEM((8, MIX), f32),
                        pltpu.VMEM((8, 512), f32), pltpu.VMEM((nslot * npair, 2 * K_C, 2 * K_C), f32)],
        compiler_params=_cp(("parallel", "arbitrary")), name="rwkv",
    )(u, u, u, u, par, mu_l, w2p, a2p, g2p, sh_r, sh_k, sh_v, sh_l, st.arr, *extra_args)


def _permute_w_in(w):
    ob, oc, og = A_COLS, A_COLS + B_COLS, A_COLS + B_COLS + R_W
    z = lambda n: jnp.zeros((w.shape[0], n), w.dtype)
    segs = [
        w[:, 0:3072],
        w[:, ob:ob + MIX],
        w[:, og:og + 3 * D_MODEL],
        w[:, oc:oc + 3 * MIX],
        w[:, ob + MIX:ob + MIX + CONV_DIM],
        w[:, oc + 3 * MIX:oc + R_W], z(512 - (R_W - 3 * MIX)),
        w[:, 3072:3072 + 2 * H_A], w[:, ob + MIX + CONV_DIM:ob + B_COLS], z(LANES - 2 * H_A - H_B),
        z(P_PAD - (C_GATE + LANES)),
    ]
    return jnp.concatenate([s.astype(bf16) for s in segs], axis=1)


def _pad_rows(w, before, total):
    return jnp.pad(w, ((before, total - before - w.shape[0]), (0, 0))).astype(bf16)


def kernel(x_prompt, x_sample, state_mlstm_C, state_mlstm_n, state_mlstm_m, state_ssm, state_conv, state_wkv, state_shift, norm_mix, w_in, mlstm_i_bias, mlstm_f_bias, mlstm_norm, mamba_conv_w, mamba_conv_b, mamba_dt_bias, mamba_A_log, mamba_D, mamba_norm, rwkv_mu, rwkv_w0, rwkv_w2, rwkv_a0, rwkv_a2, rwkv_g2, rwkv_k_k, rwkv_k_a, rwkv_r_k, rwkv_ln_w, rwkv_ln_b, w_branch_a, w_branch_b, w_branch_c, w_out, norm_ffn, ffn_w_gate, ffn_w_up, ffn_w_down, moe_router, moe_w_gate, moe_w_up, moe_w_down, norm_final):
    nb, seq, d = x_prompt.shape
    db, dseq, _ = x_sample.shape
    depth = w_in.shape[0]
    mp = nb * seq
    ms = db * dseq
    t_s = 8
    assert seq >= CONV_W - 1 and dseq >= CONV_W - 1 and dseq <= t_s

    x = jnp.concatenate([x_prompt.reshape(mp, d), x_sample.reshape(ms, d)], axis=0)
    xn = _rmsnorm(x, norm_mix[0], bf16)

    zeros_like_state = lambda s: jnp.zeros((nb,) + s.shape[2:], s.dtype)
    new_p = [[] for _ in range(7)]
    new_s = [[] for _ in range(7)]
    acc_s = (None, None, None)

    for l in range(depth):
        u = _proj_in(xn, _permute_w_in(w_in[l]))
        u_s = jnp.pad(u[mp:].reshape(db, dseq, P_PAD), ((0, 0), (0, t_s - dseq), (0, 0))).reshape(db * t_s, P_PAD)

        bias_row = jnp.zeros((1, LANES), f32)
        bias_row = bias_row.at[0, GL_I:GL_I + H_A].set(mlstm_i_bias[l]).at[0, GL_F:GL_F + H_A].set(mlstm_f_bias[l])
        bias_row = bias_row.at[0, GL_DT:GL_DT + H_B].set(mamba_dt_bias[l])

        par = jnp.zeros((16, MIX), f32)
        mu = rwkv_mu[l]
        for idx, val in ((PR_MU_R, mu[0:MIX]), (PR_MU_K, mu[MIX:2 * MIX]), (PR_MU_V, mu[2 * MIX:3 * MIX]),
                         (PR_W0, rwkv_w0[l]), (PR_A0, rwkv_a0[l]), (PR_KK, rwkv_k_k[l]), (PR_KA, rwkv_k_a[l]),
                         (PR_RK, rwkv_r_k[l].reshape(MIX)), (PR_LNW, rwkv_ln_w[l]), (PR_LNB, rwkv_ln_b[l])):
            par = par.at[idx].set(val)
        mu_l = jnp.pad(mu[3 * MIX:], (0, 512 - (R_W - 3 * MIX))).reshape(1, 512)
        w2p = _pad_rows(rwkv_w2[l], 0, 512)
        a2p = _pad_rows(rwkv_a2[l], LORA_W, 512)
        g2p = _pad_rows(rwkv_g2[l], LORA_W + LORA_A, 512)

        def mixers(uu, bt, t, t_valid, la, lb, big, accs, small):
            mn, mm, conv, shift = small
            ya, c_, n_, m_ = _mlstm(uu, bias_row, mlstm_norm[l], big[0], accs[0], mn, mm,
                                    bt=bt, t=t, rows=la, t_valid=t_valid)
            yb, h_ = _ssd(uu, bias_row, mamba_conv_w[l], mamba_conv_b[l], mamba_A_log[l], mamba_D[l],
                          mamba_norm[l], conv, big[1], accs[1], bt=bt, t=t, rows=lb, t_valid=t_valid)
            yc, s_ = _rwkv(uu, par, mu_l, w2p, a2p, g2p, shift, big[2], accs[2], bt=bt, t=t, t_valid=t_valid)
            return ya, yb, yc, (c_, h_, s_), (n_, m_)

        big_states = (state_mlstm_C, state_ssm, state_wkv)
        zero_big = tuple(_St(jnp.zeros((1, nb) + s.shape[2:], s.dtype), 0, 0, 1) for s in big_states)
        zero_small = tuple(zeros_like_state(s) for s in (state_mlstm_n, state_mlstm_m, state_conv, state_shift))
        ya_p, yb_p, yc_p, big_p, small_p = mixers(u, nb, seq, None, 128, 128, zero_big, (None, None, None),
                                                  zero_small)
        samp_big = tuple(_St(s, l, l, depth) for s in big_states)
        ya_s, yb_s, yc_s, acc_s, small_s = mixers(
            u_s, db, t_s, dseq, SAMPLE_ROWS, SAMPLE_ROWS, samp_big, acc_s,
            (state_mlstm_n[l], state_mlstm_m[l], state_conv[l], state_shift[l]))

        unpad = lambda y: y.reshape(db, t_s, MIX)[:, :dseq].reshape(ms, MIX).astype(bf16)
        ya = jnp.concatenate([ya_p, unpad(ya_s)], axis=0)
        yb = jnp.concatenate([yb_p, unpad(yb_s)], axis=0)
        yc = jnp.concatenate([yc_p, unpad(yc_s)], axis=0)

        last = lambda k: jnp.stack([u[b * seq + seq - k:(b + 1) * seq] for b in range(nb)])
        us3 = u[mp:].reshape(db, dseq, P_PAD)
        conv_rows = lambda z: z[:, -(CONV_W - 1):, C_XS:C_XS + CONV_DIM]
        shift_row = lambda z: jnp.concatenate([z[:, -1, C_R:C_R + 3 * MIX],
                                               z[:, -1, C_LORA:C_LORA + (R_W - 3 * MIX)]], axis=-1)
        zp = last(CONV_W - 1)
        new_p[0].append(big_p[0][0]); new_p[3].append(big_p[1][0]); new_p[5].append(big_p[2][0])
        for lst, small, z3 in ((new_p, small_p, zp), (new_s, small_s, us3)):
            lst[1].append(small[0]); lst[2].append(small[1])
            lst[4].append(conv_rows(z3)); lst[6].append(shift_row(z3))

        mixed = _branch_mix(ya, yb, yc, u, w_branch_a[l].astype(bf16), w_branch_b[l].astype(bf16),
                            w_branch_c[l].astype(bf16))
        x1, xn2 = _outproj(x, mixed, w_out[l].astype(bf16), norm_ffn[l])
        j = l // 2
        if l % 2 == 0:
            h = _ffn_up(xn2, ffn_w_gate[j], ffn_w_up[j])
            x = _ffn_down(h, ffn_w_down[j], x1)
        else:
            x = _moe(x1, xn2, norm_ffn[l], moe_router[j], moe_w_gate[j], moe_w_up[j], moe_w_down[j])
        if l + 1 < depth:
            xn = _rmsnorm(x, norm_mix[l + 1], bf16)

    y = _rmsnorm(x, norm_final, f32)
    y_prompt = y[:mp].reshape(nb, seq, d)
    y_sample = y[mp:].reshape(db, dseq, d)
    refs = (state_mlstm_C, state_mlstm_n, state_mlstm_m, state_ssm, state_conv, state_wkv, state_shift)
    outs_p = [jnp.stack(lst).astype(r.dtype) for lst, r in zip(new_p, refs)]
    new_s[0], new_s[3], new_s[5] = acc_s
    outs_s = [(v if i in (0, 3, 5) else jnp.stack(v)).astype(r.dtype) for i, (v, r) in enumerate(zip(new_s, refs))]
    return (y_prompt, y_sample, *outs_p, *outs_s)
```

```python
import functools

import jax
import jax.numpy as jnp
from jax import lax
from jax.experimental import pallas as pl
from jax.experimental.pallas import tpu as pltpu

f32 = jnp.float32
bf16 = jnp.bfloat16

D_MODEL = 2048
MIX = D_MODEL // 2
H_A, DK_A, DV_A = 4, 128, 256
GATE_CAP = 15.0
H_B, P_B, N_B, G_B = 16, 64, 128, 2
CONV_W = 4
CONV_DIM = MIX + 2 * G_B * N_B
H_C, K_C = 16, 64
LORA_W, LORA_A, LORA_G = 96, 96, 256
R_W = 3 * MIX + LORA_W + LORA_A + LORA_G
A_COLS = 2 * H_A * DK_A + 2 * MIX + 2 * H_A
B_COLS = MIX + CONV_DIM + H_B
D_FF = 5632
N_EXP = 8
D_FF_E = 2816
EPS = 1e-6
LN_EPS_C = 64e-5
NEG_BIG = -1e30

C_Q, C_K, C_V, C_O, C_Z, C_UG = 0, 512, 1024, 2048, 3072, 4096
C_R, C_KC, C_VC, C_XS, C_BC, C_LORA, C_GATE = 10240, 11264, 12288, 13312, 14336, 14848, 15360
P_PAD = 16384
LANES = 128
GL_I, GL_F, GL_DT = 0, H_A, 2 * H_A

VMEM_LIMIT = 56 * 1024 * 1024


def _cp(sem):
    return pltpu.CompilerParams(dimension_semantics=sem, vmem_limit_bytes=VMEM_LIMIT)


def _bf(x):
    return x.astype(bf16)


def _dot(a, b):
    return jnp.dot(_bf(a), _bf(b), preferred_element_type=f32)


def _dot_nt(a, b):
    return lax.dot_general(_bf(a), _bf(b), (((1,), (1,)), ((), ())), preferred_element_type=f32)


def _dot_tn(a, b):
    return lax.dot_general(_bf(a), _bf(b), (((0,), (0,)), ((), ())), preferred_element_type=f32)


def _split3(x):
    hi = _bf(x)
    r1 = x - hi.astype(f32)
    mid = _bf(r1)
    lo = _bf(r1 - mid.astype(f32))
    return hi, mid, lo


def _dot3(a01, x):
    hi, mid, lo = _split3(x)
    d = lambda p: jnp.dot(a01, p, preferred_element_type=f32)
    return d(hi) + d(mid) + d(lo)


def _dot3_tn(x, b01):
    hi, mid, lo = _split3(x)
    d = lambda p: lax.dot_general(p, b01, (((0,), (0,)), ((), ())), preferred_element_type=f32)
    return d(hi) + d(mid) + d(lo)


def _dotx3(a, b):
    a_hi = _bf(a)
    a_lo = _bf(a - a_hi.astype(f32))
    b_hi = _bf(b)
    b_lo = _bf(b - b_hi.astype(f32))
    d = lambda p, q: jnp.dot(p, q, preferred_element_type=f32)
    return d(a_hi, b_hi) + d(a_hi, b_lo) + d(a_lo, b_hi)


def _sigmoid(x):
    return 1.0 / (1.0 + jnp.exp(-x))


def _silu(x):
    return x * _sigmoid(x)


def _softplus(x):
    return jnp.maximum(x, 0.0) + jnp.log1p(jnp.exp(-jnp.abs(x)))


def _iota2(shape, axis):
    return lax.broadcasted_iota(jnp.int32, shape, axis)


def _tri_masks(L):
    r = _iota2((L, L), 0)
    c = _iota2((L, L), 1)
    tril_b = c <= r
    tril = jnp.where(tril_b, 1.0, 0.0).astype(bf16)
    triu = jnp.where(r <= c, 1.0, 0.0).astype(bf16)
    eye = jnp.where(r == c, 1.0, 0.0).astype(bf16)
    return tril_b, tril, triu, eye


def _rmsnorm_body(x_ref, g_ref, o_ref):
    x = x_ref[...]
    y = x * lax.rsqrt(jnp.mean(x * x, axis=-1, keepdims=True) + EPS) * g_ref[...]
    o_ref[...] = y.astype(o_ref.dtype)


def _rmsnorm(x, g, out_dtype, tm=512):
    m, d = x.shape
    return pl.pallas_call(
        _rmsnorm_body,
        grid=(m // tm,),
        in_specs=[pl.BlockSpec((tm, d), lambda i: (i, 0)), pl.BlockSpec((1, d), lambda i: (0, 0))],
        out_specs=pl.BlockSpec((tm, d), lambda i: (i, 0)),
        out_shape=jax.ShapeDtypeStruct((m, d), out_dtype),
        compiler_params=_cp(("parallel",)),
        name="rmsnorm",
    )(x, g.reshape(1, d))


def _proj_in_body(x_ref, w_ref, o_ref):
    o_ref[...] = jnp.dot(x_ref[...], w_ref[...], preferred_element_type=f32)


def _proj_in(xn, w, tm=1088, tn=1024):
    m, k = xn.shape
    n = w.shape[1]
    return pl.pallas_call(
        _proj_in_body,
        grid=(n // tn, m // tm),
        in_specs=[pl.BlockSpec((tm, k), lambda j, i: (i, 0)), pl.BlockSpec((k, tn), lambda j, i: (0, j))],
        out_specs=pl.BlockSpec((tm, tn), lambda j, i: (i, j)),
        out_shape=jax.ShapeDtypeStruct((m, n), f32),
        compiler_params=_cp(("parallel", "parallel")),
        name="proj_in",
    )(xn, w)


def _branch_body(ya_ref, yb_ref, yc_ref, ga_ref, gb_ref, gc_ref, wa_ref, wb_ref, wc_ref, o_ref):
    d = lambda y, w: jnp.dot(y[...], w[...], preferred_element_type=f32)
    acc = (_sigmoid(ga_ref[...]) * d(ya_ref, wa_ref) + _sigmoid(gb_ref[...]) * d(yb_ref, wb_ref)
           + _sigmoid(gc_ref[...]) * d(yc_ref, wc_ref))
    o_ref[...] = acc.astype(o_ref.dtype)


def _branch_mix(ya, yb, yc, u, wa, wb, wc, tm=512, tn=512):
    m = ya.shape[0]
    gb0 = C_UG // tn
    nb = D_MODEL // tn
    yspec = pl.BlockSpec((tm, MIX), lambda j, i: (i, 0))
    wspec = pl.BlockSpec((MIX, tn), lambda j, i: (0, j))
    gspec = lambda off: pl.BlockSpec((tm, tn), lambda j, i: (i, gb0 + off * nb + j))
    return pl.pallas_call(
        _branch_body,
        grid=(nb, m // tm),
        in_specs=[yspec, yspec, yspec, gspec(0), gspec(1), gspec(2), wspec, wspec, wspec],
        out_specs=pl.BlockSpec((tm, tn), lambda j, i: (i, j)),
        out_shape=jax.ShapeDtypeStruct((m, D_MODEL), bf16),
        compiler_params=_cp(("parallel", "parallel")),
        name="branch_mix",
    )(ya, yb, yc, u, u, u, wa, wb, wc)


def _outproj_body(x_ref, mx_ref, w_ref, g_ref, x1_ref, xn_ref):
    x1 = x_ref[...] + jnp.dot(mx_ref[...], w_ref[...], preferred_element_type=f32)
    x1_ref[...] = x1
    xn = x1 * lax.rsqrt(jnp.mean(x1 * x1, axis=-1, keepdims=True) + EPS) * g_ref[...]
    xn_ref[...] = xn.astype(xn_ref.dtype)


def _outproj(x, mixed, w, g, tm=256):
    m, d = x.shape
    row = pl.BlockSpec((tm, d), lambda i: (i, 0))
    return pl.pallas_call(
        _outproj_body,
        grid=(m // tm,),
        in_specs=[row, row, pl.BlockSpec((d, d), lambda i: (0, 0)), pl.BlockSpec((1, d), lambda i: (0, 0))],
        out_specs=[row, row],
        out_shape=[jax.ShapeDtypeStruct((m, d), f32), jax.ShapeDtypeStruct((m, d), bf16)],
        compiler_params=_cp(("parallel",)),
        name="outproj",
    )(x, mixed, w, g.reshape(1, d))


def _ffn_up_body(x_ref, wg_ref, wu_ref, o_ref, wgb, wub):
    @pl.when(pl.program_id(1) == 0)
    def _():
        wgb[...] = _bf(wg_ref[...])
        wub[...] = _bf(wu_ref[...])

    x = x_ref[...]
    a = jnp.dot(x, wgb[...], preferred_element_type=f32)
    b = jnp.dot(x, wub[...], preferred_element_type=f32)
    o_ref[...] = (_silu(a) * b).astype(o_ref.dtype)


def _ffn_up(xn, wg, wu, tm=512, tn=512):
    m, k = xn.shape
    n = wg.shape[1]
    wspec = pl.BlockSpec((k, tn), lambda j, i: (0, j))
    return pl.pallas_call(
        _ffn_up_body,
        grid=(n // tn, m // tm),
        in_specs=[pl.BlockSpec((tm, k), lambda j, i: (i, 0)), wspec, wspec],
        out_specs=pl.BlockSpec((tm, tn), lambda j, i: (i, j)),
        out_shape=jax.ShapeDtypeStruct((m, n), bf16),
        scratch_shapes=[pltpu.VMEM((k, tn), bf16), pltpu.VMEM((k, tn), bf16)],
        compiler_params=_cp(("parallel", "arbitrary")),
        name="ffn_up",
    )(xn, wg, wu)


def _ffn_down_body(h_ref, w_ref, x_ref, o_ref, wb):
    @pl.when(pl.program_id(1) == 0)
    def _():
        wb[...] = _bf(w_ref[...])

    o_ref[...] = x_ref[...] + jnp.dot(h_ref[...], wb[...], preferred_element_type=f32)


def _ffn_down(h, w, x, tm=256, tn=512):
    m, k = h.shape
    n = w.shape[1]
    return pl.pallas_call(
        _ffn_down_body,
        grid=(n // tn, m // tm),
        in_specs=[pl.BlockSpec((tm, k), lambda j, i: (i, 0)), pl.BlockSpec((k, tn), lambda j, i: (0, j)),
                  pl.BlockSpec((tm, tn), lambda j, i: (i, j))],
        out_specs=pl.BlockSpec((tm, tn), lambda j, i: (i, j)),
        out_shape=jax.ShapeDtypeStruct((m, n), f32),
        scratch_shapes=[pltpu.VMEM((k, tn), bf16)],
        compiler_params=_cp(("parallel", "arbitrary")),
        name="ffn_down",
    )(h, w, x)


def _router_body(x_ref, g_ref, r_ref, ti_ref, tw_ref):
    x = x_ref[...]
    xn = x * lax.rsqrt(jnp.mean(x * x, axis=-1, keepdims=True) + EPS) * g_ref[...]
    logits = jnp.dot(xn, r_ref[...], preferred_element_type=f32, precision=lax.Precision.HIGHEST)
    lane_i = _iota2(logits.shape, 1)
    lane = lane_i.astype(f32)
    lg = jnp.where(lane_i < N_EXP, logits, -jnp.inf)
    v1 = jnp.max(lg, axis=1, keepdims=True)
    i1 = jnp.min(jnp.where(lg == v1, lane, float(LANES)), axis=1, keepdims=True)
    lg2 = jnp.where(lane == i1, -jnp.inf, lg)
    v2 = jnp.max(lg2, axis=1, keepdims=True)
    i2 = jnp.min(jnp.where(lg2 == v2, lane, float(LANES)), axis=1, keepdims=True)
    e2 = jnp.exp(v2 - v1)
    w1 = 1.0 / (1.0 + e2)
    w2 = e2 / (1.0 + e2)
    ti_ref[...] = jnp.where(lane_i == 0, i1, jnp.where(lane_i == 1, i2, 0.0)).astype(jnp.int32)
    tw_ref[...] = jnp.where(lane_i == 0, w1, jnp.where(lane_i == 1, w2, 0.0))


def _router(x, g, router, tm=512):
    m, d = x.shape
    rp = jnp.zeros((d, LANES), f32).at[:, :N_EXP].set(router)
    row = pl.BlockSpec((tm, LANES), lambda i: (i, 0))
    return pl.pallas_call(
        _router_body,
        grid=(m // tm,),
        in_specs=[pl.BlockSpec((tm, d), lambda i: (i, 0)), pl.BlockSpec((1, d), lambda i: (0, 0)),
                  pl.BlockSpec((d, LANES), lambda i: (0, 0))],
        out_specs=[row, row],
        out_shape=[jax.ShapeDtypeStruct((m, LANES), jnp.int32), jax.ShapeDtypeStruct((m, LANES), f32)],
        compiler_params=_cp(("parallel",)),
        name="router",
    )(x, g.reshape(1, d), rp)


def _new_expert(te_ref, i):
    prev = te_ref[jnp.maximum(i - 1, 0)]
    return jnp.logical_or(i == 0, te_ref[i] != prev)


MOE_SUB = 256


def _for_valid_rows(tv_ref, i, tm, compute, o_ref):
    full = tv_ref[i] >= tm

    @pl.when(full)
    def _():
        compute(pl.ds(0, tm))

    for sb in range(tm // MOE_SUB):
        rows = pl.ds(sb * MOE_SUB, MOE_SUB)

        @pl.when(jnp.logical_and(jnp.logical_not(full), tv_ref[i] > sb * MOE_SUB))
        def _():
            compute(rows)

        @pl.when(jnp.logical_and(jnp.logical_not(full), tv_ref[i] <= sb * MOE_SUB))
        def _():
            o_ref[rows, :] = jnp.zeros((MOE_SUB, o_ref.shape[1]), o_ref.dtype)


def _moe_up_body(te_ref, tv_ref, x_ref, wg_ref, wu_ref, o_ref, wgb, wub):
    i = pl.program_id(1)

    @pl.when(_new_expert(te_ref, i))
    def _():
        wgb[...] = _bf(wg_ref[0])
        wub[...] = _bf(wu_ref[0])

    def compute(rows):
        x = x_ref[rows, :]
        a = jnp.dot(x, wgb[...], preferred_element_type=f32)
        b = jnp.dot(x, wub[...], preferred_element_type=f32)
        o_ref[rows, :] = (_silu(a) * b).astype(o_ref.dtype)

    _for_valid_rows(tv_ref, i, x_ref.shape[0], compute, o_ref)


def _moe_up(xg, wg, wu, te, tv, tm, tn=256):
    p, k = xg.shape
    n = wg.shape[2]
    wspec = pl.BlockSpec((1, k, tn), lambda j, i, te, tv: (te[i], 0, j))
    return pl.pallas_call(
        _moe_up_body,
        grid_spec=pltpu.PrefetchScalarGridSpec(
            num_scalar_prefetch=2,
            grid=(n // tn, p // tm),
            in_specs=[pl.BlockSpec((tm, k), lambda j, i, te, tv: (i, 0)), wspec, wspec],
            out_specs=pl.BlockSpec((tm, tn), lambda j, i, te, tv: (i, j)),
            scratch_shapes=[pltpu.VMEM((k, tn), bf16), pltpu.VMEM((k, tn), bf16)],
        ),
        out_shape=jax.ShapeDtypeStruct((p, n), bf16),
        compiler_params=_cp(("parallel", "arbitrary")),
        name="moe_up",
    )(te, tv, xg, wg, wu)


def _moe_down_body(te_ref, tv_ref, h_ref, w_ref, rw_ref, o_ref, wb):
    i = pl.program_id(1)

    @pl.when(_new_expert(te_ref, i))
    def _():
        wb[...] = _bf(w_ref[0])

    def compute(rows):
        o_ref[rows, :] = jnp.dot(h_ref[rows, :], wb[...], preferred_element_type=f32) * rw_ref[rows, 0:1]

    _for_valid_rows(tv_ref, i, h_ref.shape[0], compute, o_ref)


def _moe_down(hg, wd, roww, te, tv, tm, tn=512):
    p, k = hg.shape
    n = wd.shape[2]
    return pl.pallas_call(
        _moe_down_body,
        grid_spec=pltpu.PrefetchScalarGridSpec(
            num_scalar_prefetch=2,
            grid=(n // tn, p // tm),
            in_specs=[pl.BlockSpec((tm, k), lambda j, i, te, tv: (i, 0)),
                      pl.BlockSpec((1, k, tn), lambda j, i, te, tv: (te[i], 0, j)),
                      pl.BlockSpec((tm, LANES), lambda j, i, te, tv: (i, 0))],
            out_specs=pl.BlockSpec((tm, tn), lambda j, i, te, tv: (i, j)),
            scratch_shapes=[pltpu.VMEM((k, tn), bf16)],
        ),
        out_shape=jax.ShapeDtypeStruct((p, n), f32),
        compiler_params=_cp(("parallel", "arbitrary")),
        name="moe_down",
    )(te, tv, hg, wd, roww)


def _moe(x1, xn_bf, g, router, wg, wu, wd, tm=512):
    m = x1.shape[0]
    top_i, top_w = _router(x1, g, router)
    e_flat = top_i[:, :2].reshape(-1)
    w_flat = top_w[:, :2].reshape(-1)
    npair = 2 * m
    ntile = npair // tm + N_EXP
    ptot = ntile * tm
    experts = jnp.arange(N_EXP, dtype=jnp.int32)
    counts = jnp.sum((e_flat[:, None] == experts[None, :]).astype(jnp.int32), axis=0)
    tiles_per = (counts + tm - 1) // tm
    tile_end = jnp.cumsum(tiles_per)
    fill_end = jnp.cumsum(tiles_per * tm - counts)
    fidx = jnp.arange(ptot - npair, dtype=jnp.int32)
    fill_key = jnp.sum((fidx[:, None] >= fill_end[None, :]).astype(jnp.int32), axis=1)
    keys = jnp.concatenate([e_flat, fill_key])
    ids = jnp.arange(ptot, dtype=jnp.int32)
    wts = jnp.concatenate([w_flat, jnp.zeros((ptot - npair,), f32)])
    _, id_s, roww = lax.sort((keys, ids, wts), num_keys=1, is_stable=True)
    src_tok = jnp.where(id_s < npair, id_s >> 1, 0)
    _, slot_of = lax.sort((id_s, ids), num_keys=1)
    pos = slot_of[:npair].reshape(m, 2)
    tidx = jnp.arange(ntile, dtype=jnp.int32)
    te = jnp.minimum(jnp.sum((tidx[:, None] >= tile_end[None, :]).astype(jnp.int32), axis=1), N_EXP - 1)
    first_tile = (tile_end - tiles_per)[te]
    tv = jnp.clip(counts[te] - (tidx - first_tile) * tm, 0, tm)
    tv = jnp.where(tidx < tile_end[-1], tv, 0).astype(jnp.int32)
    te = jnp.where(tv > 0, te, te[jnp.maximum(tile_end[-1] - 1, 0)]).astype(jnp.int32)

    d = xn_bf.shape[1]
    xg = jnp.take(xn_bf.reshape(m, d // LANES, LANES), src_tok, axis=0, mode="clip").reshape(ptot, d)
    hg = _moe_up(xg, wg, wu, te, tv, tm)
    yg = _moe_down(hg, wd, jnp.broadcast_to(roww[:, None], (ptot, LANES)), te, tv, tm)
    return x1 + jnp.take(yg, pos[:, 0], axis=0, mode="clip") + jnp.take(yg, pos[:, 1], axis=0, mode="clip")


def _block_masks(R, lsub):
    lg = lsub.bit_length() - 1
    r = _iota2((R, R), 0)
    c = _iota2((R, R), 1)
    same = (r >> lg) == (c >> lg)
    tril_b = jnp.logical_and(same, c <= r)
    one = lambda m: jnp.where(m, 1.0, 0.0).astype(bf16)
    return tril_b, one(tril_b), one(jnp.logical_and(same, r <= c)), one(r == c)


def _per_seq_col(vals, lsub):
    return jnp.concatenate([jnp.broadcast_to(v, (lsub, 1)) for v in vals], axis=0)


def _mlstm_body(*refs, R, nslot, t_valid, n_in):
    q_ref, k_ref, v_ref, o_ref, g_ref, bias_ref, ng_ref, c0_ref, n0_ref, m0_ref = refs[:10]
    y_ref, c_ref, n_ref, m_ref = refs[n_in:n_in + 4]
    c = pl.program_id(1)
    lsub = R // nslot
    slots = range(nslot)
    heads = range(H_A)
    rows = [slice(s * lsub, (s + 1) * lsub) for s in slots]

    @pl.when(c == 0)
    def _():
        c_ref[...] = c0_ref[...]
        n_ref[...] = n0_ref[...]
        m_ref[...] = m0_ref[...]

    tril_b, tril, triu, eye = _block_masks(R, lsub)
    tg = GATE_CAP * jnp.tanh((g_ref[...] + bias_ref[...]) / GATE_CAP)
    li_all = tg
    lf_all = jnp.minimum(tg, 0.0) - jnp.log1p(jnp.exp(-jnp.abs(tg)))
    if t_valid is not None:
        valid = (_iota2((R, LANES), 0) & (lsub - 1)) < t_valid
        li_all = jnp.where(valid, li_all, NEG_BIG)
        lf_all = jnp.where(valid, lf_all, 0.0)
    b_all = _dot3(tril, lf_all)
    b_rows = _dot3_tn(lf_all, triu)
    li_rows = _dot3_tn(li_all, eye)

    q = [q_ref[:, h * DK_A:(h + 1) * DK_A] for h in heads]
    k = [k_ref[:, h * DK_A:(h + 1) * DK_A] * (DK_A ** -0.5) for h in heads]
    v = [v_ref[:, h * DV_A:(h + 1) * DV_A] for h in heads]
    qb, kb, vb = [_bf(z) for z in q], [_bf(z) for z in k], [_bf(z) for z in v]
    bc = [b_all[:, GL_F + h:GL_F + h + 1] for h in heads]
    lc = [li_all[:, GL_I + h:GL_I + h + 1] for h in heads]
    m_prev = [[m_ref[s, h:h + 1, 0:1] for s in slots] for h in heads]
    c_prev = [[c_ref[s, h] for s in slots] for h in heads]
    n_prev = [[n_ref[s, h:h + 1, :] for s in slots] for h in heads]

    qk = [_dot_nt(qb[h], kb[h]) for h in heads]
    inter = [jnp.concatenate([_dot(q[h][rows[s]], c_prev[h][s]) for s in slots], axis=0) for h in heads]
    hh = []
    for h in heads:
        br = b_rows[GL_F + h:GL_F + h + 1, :]
        lr = li_rows[GL_I + h:GL_I + h + 1, :]
        dmat = jnp.where(tril_b, bc[h] - br + lr, -jnp.inf)
        g_inter = bc[h] + _per_seq_col(m_prev[h], lsub)
        m_t = jnp.maximum(g_inter, jnp.max(dmat, axis=1, keepdims=True))
        w_inter = jnp.exp(g_inter - m_t)
        s_mat = qk[h] * jnp.exp(dmat - m_t)
        n_rows = jnp.concatenate([jnp.broadcast_to(n_prev[h][s], (lsub, DK_A)) for s in slots], axis=0)
        num = w_inter * inter[h] + _dot(s_mat, vb[h])
        den = w_inter * jnp.sum(q[h] * n_rows, axis=1, keepdims=True) + jnp.sum(s_mat, axis=1, keepdims=True)
        hh.append(num / jnp.maximum(jnp.abs(den), jnp.exp(-m_t)))

    for h in heads:
        for s in slots:
            bcs, mp = bc[h][rows[s]], m_prev[h][s]
            b_last = bcs[lsub - 1:lsub, :]
            g_s = b_last - bcs + lc[h][rows[s]]
            m_new = jnp.maximum(b_last + mp, jnp.max(g_s, axis=0, keepdims=True))
            decay = jnp.exp(b_last + mp - m_new)
            ak = jnp.exp(g_s - m_new) * k[h][rows[s]]
            c_ref[s, h] = decay * c_prev[h][s] + _dot_tn(ak, v[h][rows[s]])
            n_ref[s, h:h + 1, :] = decay * n_prev[h][s] + jnp.sum(ak, axis=0, keepdims=True)
            m_ref[s, h:h + 1, :] = jnp.broadcast_to(m_new, (1, LANES))

    for h in heads:
        cols = slice(h * DV_A, (h + 1) * DV_A)
        hn = hh[h] * lax.rsqrt(jnp.mean(hh[h] * hh[h], axis=-1, keepdims=True) + EPS) * ng_ref[:, cols]
        y_ref[:, cols] = (hn * _sigmoid(o_ref[:, cols])).astype(y_ref.dtype)


def _state_io(tail, st, acc, nslot):
    zeros = (0,) * len(tail)
    in_spec = pl.BlockSpec((None, nslot) + tail, lambda b, c: (st.li, b) + zeros)
    out_spec = pl.BlockSpec((None, nslot) + tail, lambda b, c: (st.lo, b) + zeros)
    extra_in = [] if acc is None else [pl.BlockSpec(memory_space=pl.ANY)]
    extra_args = [] if acc is None else [acc]
    return in_spec, out_spec, extra_in, extra_args


def _tiling(bt, t, rows):
    nslot = 1 if t >= rows else rows // t
    return nslot, max(t // rows, 1), bt // nslot


class _St:
    def __init__(self, arr, li, lo, depth):
        self.arr, self.li, self.lo, self.depth = arr, li, lo, depth


def _mlstm(u, bias_row, norm_g, st, acc, n0, m0, *, bt, t, rows, t_valid):
    nslot, nc, nblk = _tiling(bt, t, rows)
    R = rows
    rb = lambda b, c: b * nc + c
    m0b = jnp.broadcast_to(m0[:, :, None], (bt, H_A, LANES))
    c_in, c_out, extra_in, extra_args = _state_io((H_A, DK_A, DV_A), st, acc, nslot)
    in_specs = [
        pl.BlockSpec((R, 512), lambda b, c: (rb(b, c), C_Q // 512)),
        pl.BlockSpec((R, 512), lambda b, c: (rb(b, c), C_K // 512)),
        pl.BlockSpec((R, MIX), lambda b, c: (rb(b, c), C_V // MIX)),
        pl.BlockSpec((R, MIX), lambda b, c: (rb(b, c), C_O // MIX)),
        pl.BlockSpec((R, LANES), lambda b, c: (rb(b, c), C_GATE // LANES)),
        pl.BlockSpec((1, LANES), lambda b, c: (0, 0)),
        pl.BlockSpec((1, MIX), lambda b, c: (0, 0)),
        c_in,
        pl.BlockSpec((nslot, H_A, DK_A), lambda b, c: (b, 0, 0)),
        pl.BlockSpec((nslot, H_A, LANES), lambda b, c: (b, 0, 0)),
    ] + extra_in
    out_specs = [
        pl.BlockSpec((R, MIX), lambda b, c: (rb(b, c), 0)),
        c_out,
        pl.BlockSpec((nslot, H_A, DK_A), lambda b, c: (b, 0, 0)),
        pl.BlockSpec((nslot, H_A, LANES), lambda b, c: (b, 0, 0)),
    ]
    out_shape = [
        jax.ShapeDtypeStruct((bt * t, MIX), bf16),
        jax.ShapeDtypeStruct((st.depth, bt, H_A, DK_A, DV_A), f32),
        jax.ShapeDtypeStruct((bt, H_A, DK_A), f32),
        jax.ShapeDtypeStruct((bt, H_A, LANES), f32),
    ]
    y, c_new, n_new, m_new = pl.pallas_call(
        functools.partial(_mlstm_body, R=R, nslot=nslot, t_valid=t_valid, n_in=len(in_specs)),
        grid=(nblk, nc), in_specs=in_specs, out_specs=out_specs, out_shape=out_shape,
        input_output_aliases={} if acc is None else {len(in_specs) - 1: 1},
        compiler_params=_cp(("parallel", "arbitrary")), name="mlstm",
    )(u, u, u, u, u, bias_row, norm_g.reshape(1, MIX), st.arr, n0, m0b, *extra_args)
    return y, c_new, n_new, m_new[:, :, 0]


def _ssd_body(*refs, R, nslot, t_valid, n_in):
    (z_ref, xs_ref, bcr_ref, g_ref, bias_ref, cwx_ref, cwb_ref, cbx_ref, cbb_ref, alog_ref,
     d_ref, ng_ref, cx0_ref, cb0_ref, h0_ref) = refs[:15]
    y_ref, h_ref, px_sc, pb_sc = refs[n_in:n_in + 4]
    c = pl.program_id(1)
    lsub = R // nslot
    slots = range(nslot)
    rows = [slice(s * lsub, (s + 1) * lsub) for s in slots]

    @pl.when(c == 0)
    def _():
        h_ref[...] = h0_ref[...]
        if nslot == 1:
            px_sc[...] = jnp.zeros(px_sc.shape, f32)
            pb_sc[...] = jnp.zeros(pb_sc.shape, f32)
            px_sc[R - 8:R, :] = cx0_ref[...]
            pb_sc[R - 8:R, :] = cb0_ref[...]

    def conv(cur, prev, cw_ref, cb_ref):
        tpos = _iota2(cur.shape, 0) & (lsub - 1)
        acc = cb_ref[...] + cw_ref[CONV_W - 1:CONV_W, :] * cur
        for sft in range(1, CONV_W):
            shifted = jnp.where(tpos >= sft, pltpu.roll(cur, sft, 0), pltpu.roll(prev, sft, 0))
            acc = acc + cw_ref[CONV_W - 1 - sft:CONV_W - sft, :] * shifted
        return _silu(acc)

    cur_x = xs_ref[...]
    cur_b = bcr_ref[...]
    prev_x = px_sc[...] if nslot == 1 else cx0_ref[...]
    prev_b = pb_sc[...] if nslot == 1 else cb0_ref[...]
    xs = conv(cur_x, prev_x, cwx_ref, cbx_ref)
    bcm = conv(cur_b, prev_b, cwb_ref, cbb_ref)
    if nslot == 1:
        px_sc[...] = cur_x
        pb_sc[...] = cur_b

    tril_b, tril, triu, eye = _block_masks(R, lsub)
    dt_all = _softplus(g_ref[...] + bias_ref[...])
    if t_valid is not None:
        valid = (_iota2((R, LANES), 0) & (lsub - 1)) < t_valid
        dt_all = jnp.where(valid, dt_all, 0.0)
    da_all = dt_all * (-jnp.exp(alog_ref[...]))
    acs_all = _dot3(tril, da_all)
    acs_rows = _dot3_tn(da_all, triu)
    dt_rows = _dot3_tn(dt_all, eye)

    lane0 = _iota2((R, LANES), 1) < P_B
    row0 = _iota2((2 * P_B, 1), 0) < P_B
    npair = H_B // 2
    bm = [bcm[:, g * N_B:(g + 1) * N_B] for g in range(G_B)]
    cm = [bcm[:, (G_B + g) * N_B:(G_B + g + 1) * N_B] for g in range(G_B)]
    cb = [_dot_nt(cm[g], bm[g]) for g in range(G_B)]

    def pair_group(pairs):
        grp = {p: (2 * p) // (H_B // G_B) for p in pairs}
        xp = {p: xs[:, p * LANES:(p + 1) * LANES] for p in pairs}
        hp = {p: [h_ref[s, 2 * p:2 * p + 2].reshape(2 * P_B, N_B) for s in slots] for p in pairs}
        wts, acs, tails, alast = {}, {}, {}, {}
        for hd in [2 * p + i for p in pairs for i in (0, 1)]:
            ln = GL_DT + hd
            ac = acs_all[:, ln:ln + 1]
            seg = jnp.where(tril_b, ac - acs_rows[ln:ln + 1, :], -jnp.inf)
            wts[hd] = _bf(cb[grp[hd // 2]] * jnp.exp(seg) * dt_rows[ln:ln + 1, :])
            al = [ac[(s + 1) * lsub - 1:(s + 1) * lsub, :] for s in slots]
            acs[hd], alast[hd] = ac, al
            tails[hd] = jnp.exp(_per_seq_col(al, lsub) - ac) * dt_all[:, ln:ln + 1]
        xpb = {p: _bf(xp[p]) for p in pairs}
        intra = {p: jnp.where(lane0, jnp.dot(wts[2 * p], xpb[p], preferred_element_type=f32),
                              jnp.dot(wts[2 * p + 1], xpb[p], preferred_element_type=f32)) for p in pairs}
        inter = {p: jnp.concatenate([_dot_nt(cm[grp[p]][rows[s]], hp[p][s]) for s in slots], axis=0) for p in pairs}
        for p in pairs:
            tailx = xp[p] * jnp.where(lane0, tails[2 * p], tails[2 * p + 1])
            for s in slots:
                dec = jnp.where(row0, jnp.exp(alast[2 * p][s]), jnp.exp(alast[2 * p + 1][s]))
                new = dec * hp[p][s] + _dot_tn(tailx[rows[s]], bm[grp[p]][rows[s]])
                h_ref[s, 2 * p:2 * p + 2] = new.reshape(2, P_B, N_B)
        return [intra[p] + inter[p] * jnp.where(lane0, jnp.exp(acs[2 * p]), jnp.exp(acs[2 * p + 1]))
                + d_ref[:, p * LANES:(p + 1) * LANES] * xp[p] for p in pairs]

    gsz = 1 if nslot == 1 else npair
    ys = []
    for p0 in range(0, npair, gsz):
        ys += pair_group(list(range(p0, p0 + gsz)))

    y = jnp.concatenate(ys, axis=1) * _silu(z_ref[...])
    gw = MIX // G_B
    outs = []
    for grp in range(G_B):
        sg = y[:, grp * gw:(grp + 1) * gw]
        outs.append(sg * lax.rsqrt(jnp.mean(sg * sg, axis=-1, keepdims=True) + EPS))
    y_ref[...] = (jnp.concatenate(outs, axis=1) * ng_ref[...]).astype(y_ref.dtype)


def _ssd(u, bias_row, conv_w, conv_b, a_log, d_skip, norm_g, conv0, st, acc, *, bt, t, rows, t_valid):
    nslot, nc, nblk = _tiling(bt, t, rows)
    R = rows
    lsub = R // nslot
    rb = lambda b, c: b * nc + c
    const = lambda shape: pl.BlockSpec(shape, lambda b, c: tuple(0 for _ in shape))
    alog_row = jnp.zeros((1, LANES), f32).at[0, GL_DT:GL_DT + H_B].set(a_log)
    d_row = jnp.repeat(d_skip, P_B).reshape(1, MIX)
    crows = 8 if nslot == 1 else lsub
    if nslot > 1:
        conv0 = jnp.roll(conv0.reshape(nblk, nslot, CONV_W - 1, CONV_DIM), -1, axis=1).reshape(bt, CONV_W - 1, CONV_DIM)
    conv0p = jnp.pad(conv0, ((0, 0), (crows - (CONV_W - 1), 0), (0, 0))).reshape(bt * crows, CONV_DIM)
    cblk = 8 if nslot == 1 else R
    h_in, h_out, extra_in, extra_args = _state_io((H_B, P_B, N_B), st, acc, nslot)
    in_specs = [
        pl.BlockSpec((R, MIX), lambda b, c: (rb(b, c), C_Z // MIX)),
        pl.BlockSpec((R, MIX), lambda b, c: (rb(b, c), C_XS // MIX)),
        pl.BlockSpec((R, 512), lambda b, c: (rb(b, c), C_BC // 512)),
        pl.BlockSpec((R, LANES), lambda b, c: (rb(b, c), C_GATE // LANES)),
        const((1, LANES)), const((CONV_W, MIX)), const((CONV_W, 512)), const((1, MIX)), const((1, 512)),
        const((1, LANES)), const((1, MIX)), const((1, MIX)),
        pl.BlockSpec((cblk, MIX), lambda b, c: (b, 0)),
        pl.BlockSpec((cblk, 512), lambda b, c: (b, 0)),
        h_in,
    ] + extra_in
    out_specs = [pl.BlockSpec((R, MIX), lambda b, c: (rb(b, c), 0)), h_out]
    out_shape = [jax.ShapeDtypeStruct((bt * t, MIX), bf16), jax.ShapeDtypeStruct((st.depth, bt, H_B, P_B, N_B), f32)]
    return pl.pallas_call(
        functools.partial(_ssd_body, R=R, nslot=nslot, t_valid=t_valid, n_in=len(in_specs)),
        grid=(nblk, nc), in_specs=in_specs, out_specs=out_specs, out_shape=out_shape,
        input_output_aliases={} if acc is None else {len(in_specs) - 1: 1},
        scratch_shapes=[pltpu.VMEM((R, MIX), f32), pltpu.VMEM((R, 512), f32)],
        compiler_params=_cp(("parallel", "arbitrary")), name="ssd",
    )(u, u, u, u, bias_row, conv_w[:, :MIX], conv_w[:, MIX:], conv_b[:MIX].reshape(1, MIX),
      conv_b[MIX:].reshape(1, 512), alog_row, d_row, norm_g.reshape(1, MIX),
      conv0p[:, :MIX], conv0p[:, MIX:], st.arr, *extra_args)


PR_MU_R, PR_MU_K, PR_MU_V, PR_W0, PR_A0, PR_KK, PR_KA, PR_RK, PR_LNW, PR_LNB = range(10)


def _rwkv_body(*refs, R, nslot, t_valid, n_in):
    (r_ref, k_ref, v_ref, l_ref, par_ref, mul_ref, w2_ref, a2_ref, g2_ref,
     shr_ref, shk_ref, shv_ref, shl_ref, s0_ref) = refs[:14]
    y_ref, s_ref, cr_sc, ck_sc, cv_sc, cl_sc, st_sc = refs[n_in:n_in + 7]
    c = pl.program_id(1)
    nc = pl.num_programs(1)
    npair = H_C // 2
    lsub = R // nslot
    lg = lsub.bit_length() - 1
    L2 = 2 * R

    @pl.when(c == 0)
    def _():
        if nslot == 1:
            cr_sc[...] = shr_ref[...]
            ck_sc[...] = shk_ref[...]
            cv_sc[...] = shv_ref[...]
            cl_sc[...] = shl_ref[...]
        rowm = _iota2((2 * K_C, K_C), 0) < K_C
        for s in range(nslot):
            for p in range(npair):
                x = s0_ref[s, 2 * p:2 * p + 2].reshape(2 * K_C, K_C)
                st_sc[s * npair + p] = jnp.concatenate([jnp.where(rowm, x, 0.0), jnp.where(rowm, 0.0, x)], axis=1)

    par = lambda i: par_ref[i:i + 1, :]

    def shifted(cur_ref, carry, sh_ref, mu):
        cur = cur_ref[...]
        first = (_iota2(cur.shape, 0) & (lsub - 1)) == 0
        if nslot == 1:
            before = carry[0:1, :]
            carry[0:1, :] = cur[R - 1:R, :]
        else:
            before = sh_ref[...]
        prev = jnp.where(first, before, pltpu.roll(cur, 1, 0))
        return cur + (prev - cur) * mu

    xr = shifted(r_ref, cr_sc, shr_ref, par(PR_MU_R))
    xk = shifted(k_ref, ck_sc, shk_ref, par(PR_MU_K))
    xv = shifted(v_ref, cv_sc, shv_ref, par(PR_MU_V))
    xl = shifted(l_ref, cl_sc, shl_ref, mul_ref[...])

    lw = jnp.dot(_bf(jnp.tanh(xl)), w2_ref[...], preferred_element_type=f32)
    la = jnp.dot(_bf(xl), a2_ref[...], preferred_element_type=f32)
    g = jnp.dot(_bf(_sigmoid(xl)), g2_ref[...], preferred_element_type=f32)
    wlog = -_softplus(-(par(PR_W0) + lw)) - 0.5
    logdec = -jnp.exp(wlog)
    a = _sigmoid(par(PR_A0) + la)
    kkr = xk * par(PR_KK)
    k2 = xk * (1.0 + (a - 1.0) * par(PR_KA))
    if t_valid is not None:
        valid = (_iota2((R, MIX), 0) & (lsub - 1)) < t_valid
        logdec = jnp.where(valid, logdec, 0.0)
        kkr = jnp.where(valid, kkr, 0.0)
        k2 = jnp.where(valid, k2, 0.0)
        xv = jnp.where(valid, xv, 0.0)

    rr = _iota2((R, R), 0)
    cc = _iota2((R, R), 1)
    tril = jnp.where(jnp.logical_and(cc <= rr, (rr >> lg) == (cc >> lg)), 1.0, 0.0).astype(bf16)
    cum = _dot3(tril, logdec)
    pfull = jnp.exp(cum)
    pinv = jnp.exp(-cum)
    pprev = jnp.exp(cum - logdec)

    lane0 = _iota2((R, LANES), 1) < K_C
    r2 = _iota2((L2, L2), 0)
    c2 = _iota2((L2, L2), 1)
    same = (r2 >> lg) == (c2 >> lg)
    strict = jnp.logical_and(same, (c2 & (lsub - 1)) < (r2 & (lsub - 1)))
    incl = jnp.logical_and(same, (c2 & (lsub - 1)) <= (r2 & (lsub - 1)))
    eye2 = jnp.where(r2 == c2, 1.0, 0.0)

    def hsum(x):
        s0 = jnp.sum(jnp.where(lane0, x, 0.0), axis=1, keepdims=True)
        s1 = jnp.sum(jnp.where(lane0, 0.0, x), axis=1, keepdims=True)
        return jnp.where(lane0, s0, s1)

    def stack(x):
        h0 = jnp.where(lane0, x, 0.0)
        h1 = jnp.where(lane0, 0.0, x)
        pieces = []
        for s in range(nslot):
            pieces += [h0[s * lsub:(s + 1) * lsub], h1[s * lsub:(s + 1) * lsub]]
        return jnp.concatenate(pieces, axis=0)

    nsq = max(lg - 1, 0)
    w = 2 * lsub
    pairs = range(npair)
    sls = [slice(p * LANES, (p + 1) * LANES) for p in pairs]
    s_old = [[st_sc[s * npair + p] for s in range(nslot)] for p in pairs]
    kap2, bt2, kt2, rt2, v2 = [], [], [], [], []
    for p in pairs:
        sl = sls[p]
        kkp = kkr[:, sl]
        kap = kkp / jnp.maximum(jnp.sqrt(hsum(kkp * kkp)), 1e-12)
        kap2.append(_bf(stack(kap * pprev[:, sl])))
        bt2.append(stack(kap * a[:, sl] * pinv[:, sl]))
        kt2.append(stack(k2[:, sl] * pinv[:, sl]))
        rt2.append(_bf(stack(xr[:, sl] * pfull[:, sl])))
        v2.append(_bf(stack(xv[:, sl])))
    gram = [_dot_nt(jnp.concatenate([kap2[p], rt2[p]], axis=0),
                    jnp.concatenate([_bf(bt2[p]), _bf(kt2[p])], axis=0)) for p in pairs]
    n_kb = [jnp.where(strict, gram[p][0:L2, 0:L2], 0.0) for p in pairs]
    a_kr = [jnp.concatenate([jnp.where(strict, gram[p][0:L2, L2:2 * L2], 0.0),
                             jnp.where(incl, gram[p][L2:2 * L2, L2:2 * L2], 0.0)], axis=0) for p in pairs]
    a_rb = [jnp.where(incl, gram[p][L2:2 * L2, 0:L2], 0.0) for p in pairs]
    tinv = [eye2 - n_kb[p] for p in pairs]
    if nsq > 0:
        pw = [_dot(n_kb[p], n_kb[p]) for p in pairs]
        for it in range(nsq):
            if it + 1 < nsq:
                both = [_dot(jnp.concatenate([pw[p], tinv[p]], axis=0), pw[p]) for p in pairs]
                pw = [both[p][0:L2] for p in pairs]
                tinv = [tinv[p] + both[p][L2:2 * L2] for p in pairs]
            else:
                tinv = [tinv[p] + _dot(tinv[p], pw[p]) for p in pairs]
    ks = [[_dot_nt(jnp.concatenate([kap2[p][s * w:(s + 1) * w], rt2[p][s * w:(s + 1) * w]], axis=0), s_old[p][s])
           for s in range(nslot)] for p in pairs]
    av = [_dot(a_kr[p], v2[p]) for p in pairs]
    u2 = [_dot(tinv[p], jnp.concatenate([z[0:w] for z in ks[p]], axis=0) + av[p][0:L2]) for p in pairs]
    au = [_dot(a_rb[p], u2[p]) for p in pairs]
    for p in pairs:
        u2b = _bf(u2[p])
        for s in range(nslot):
            pl_row = pfull[(s + 1) * lsub - 1:(s + 1) * lsub, sls[p]]
            lhs = jnp.concatenate([v2[p][s * w:(s + 1) * w], -u2b[s * w:(s + 1) * w]], axis=0)
            rhs = jnp.concatenate([kt2[p][s * w:(s + 1) * w], bt2[p][s * w:(s + 1) * w]], axis=0) * pl_row
            st_sc[s * npair + p] = s_old[p][s] * pl_row + _dot_tn(lhs, rhs)
    for p in pairs:
        sl = sls[p]
        y2 = jnp.concatenate([z[w:2 * w] for z in ks[p]], axis=0) + av[p][L2:2 * L2] - au[p]
        yp = jnp.concatenate([y2[s * w:s * w + lsub] + y2[s * w + lsub:(s + 1) * w] for s in range(nslot)], axis=0)
        mean = hsum(yp) * (1.0 / K_C)
        dy = yp - mean
        var = hsum(dy * dy) * (1.0 / K_C)
        yn = dy * lax.rsqrt(var + LN_EPS_C) * par(PR_LNW)[:, sl] + par(PR_LNB)[:, sl]
        bonus = hsum(xr[:, sl] * k2[:, sl] * par(PR_RK)[:, sl]) * xv[:, sl]
        y_ref[:, sl] = ((yn + bonus) * g[:, sl]).astype(y_ref.dtype)

    @pl.when(c == nc - 1)
    def _():
        for s in range(nslot):
            for p in range(npair):
                blk = st_sc[s * npair + p]
                s_ref[s, 2 * p] = blk[0:K_C, 0:K_C]
                s_ref[s, 2 * p + 1] = blk[K_C:2 * K_C, K_C:2 * K_C]


RWKV_ROWS = 64
SAMPLE_ROWS = 64


def _rwkv(u, par, mu_l, w2p, a2p, g2p, shift0, st, acc, *, bt, t, t_valid):
    R = RWKV_ROWS
    nslot = 1 if t >= R else R // t
    nc = max(t // R, 1)
    nblk = bt // nslot
    rb = lambda b, c: b * nc + c
    const = lambda shape: pl.BlockSpec(shape, lambda b, c: tuple(0 for _ in shape))
    npair = H_C // 2
    lsub = R // nslot
    sh_rows = 8 if nslot == 1 else lsub
    padr = lambda x: jnp.pad(x[:, None, :], ((0, 0), (0, sh_rows - 1), (0, 0))).reshape(bt * sh_rows, x.shape[-1])
    sh_r = padr(shift0[:, 0:MIX])
    sh_k = padr(shift0[:, MIX:2 * MIX])
    sh_v = padr(shift0[:, 2 * MIX:3 * MIX])
    sh_l = padr(jnp.pad(shift0[:, 3 * MIX:], ((0, 0), (0, 512 - (R_W - 3 * MIX)))))
    shb = 8 if nslot == 1 else R
    tail = (H_C, K_C, K_C)
    zeros = (0,) * len(tail)
    s_in = pl.BlockSpec((None, nslot) + tail, lambda b, c: (st.li, b) + zeros)
    s_out = pl.BlockSpec((None, nslot) + tail, lambda b, c: (st.lo, b) + zeros)
    extra_in = [] if acc is None else [pl.BlockSpec(memory_space=pl.ANY)]
    extra_args = [] if acc is None else [acc]
    in_specs = [
        pl.BlockSpec((R, MIX), lambda b, c: (rb(b, c), C_R // MIX)),
        pl.BlockSpec((R, MIX), lambda b, c: (rb(b, c), C_KC // MIX)),
        pl.BlockSpec((R, MIX), lambda b, c: (rb(b, c), C_VC // MIX)),
        pl.BlockSpec((R, 512), lambda b, c: (rb(b, c), C_LORA // 512)),
        const((16, MIX)), const((1, 512)), const((512, MIX)), const((512, MIX)), const((512, MIX)),
        pl.BlockSpec((shb, MIX), lambda b, c: (b, 0)),
        pl.BlockSpec((shb, MIX), lambda b, c: (b, 0)),
        pl.BlockSpec((shb, MIX), lambda b, c: (b, 0)),
        pl.BlockSpec((shb, 512), lambda b, c: (b, 0)),
        s_in,
    ] + extra_in
    out_specs = [pl.BlockSpec((R, MIX), lambda b, c: (rb(b, c), 0)), s_out]
    out_shape = [jax.ShapeDtypeStruct((bt * t, MIX), bf16), jax.ShapeDtypeStruct((st.depth, bt) + tail, f32)]
    return pl.pallas_call(
        functools.partial(_rwkv_body, R=R, nslot=nslot, t_valid=t_valid, n_in=len(in_specs)),
        grid=(nblk, nc), in_specs=in_specs, out_specs=out_specs, out_shape=out_shape,
        input_output_aliases={} if acc is None else {len(in_specs) - 1: 1},
        scratch_shapes=[pltpu.VMEM((8, MIX), f32), pltpu.VMEM((8, MIX), f32), pltpu.VMEM((8, MIX), f32),
                        pltpu.VMEM((8, 512), f32), pltpu.VMEM((nslot * npair, 2 * K_C, 2 * K_C), f32)],
        compiler_params=_cp(("parallel", "arbitrary")), name="rwkv",
    )(u, u, u, u, par, mu_l, w2p, a2p, g2p, sh_r, sh_k, sh_v, sh_l, st.arr, *extra_args)


def _permute_w_in(w):
    ob, oc, og = A_COLS, A_COLS + B_COLS, A_COLS + B_COLS + R_W
    z = lambda n: jnp.zeros((w.shape[0], n), w.dtype)
    segs = [
        w[:, 0:3072],
        w[:, ob:ob + MIX],
        w[:, og:og + 3 * D_MODEL],
        w[:, oc:oc + 3 * MIX],
        w[:, ob + MIX:ob + MIX + CONV_DIM],
        w[:, oc + 3 * MIX:oc + R_W], z(512 - (R_W - 3 * MIX)),
        w[:, 3072:3072 + 2 * H_A], w[:, ob + MIX + CONV_DIM:ob + B_COLS], z(LANES - 2 * H_A - H_B),
        z(P_PAD - (C_GATE + LANES)),
    ]
    return jnp.concatenate([s.astype(bf16) for s in segs], axis=1)


def _pad_rows(w, before, total):
    return jnp.pad(w, ((before, total - before - w.shape[0]), (0, 0))).astype(bf16)


def kernel(x_prompt, x_sample, state_mlstm_C, state_mlstm_n, state_mlstm_m, state_ssm, state_conv, state_wkv, state_shift, norm_mix, w_in, mlstm_i_bias, mlstm_f_bias, mlstm_norm, mamba_conv_w, mamba_conv_b, mamba_dt_bias, mamba_A_log, mamba_D, mamba_norm, rwkv_mu, rwkv_w0, rwkv_w2, rwkv_a0, rwkv_a2, rwkv_g2, rwkv_k_k, rwkv_k_a, rwkv_r_k, rwkv_ln_w, rwkv_ln_b, w_branch_a, w_branch_b, w_branch_c, w_out, norm_ffn, ffn_w_gate, ffn_w_up, ffn_w_down, moe_router, moe_w_gate, moe_w_up, moe_w_down, norm_final):
    nb, seq, d = x_prompt.shape
    db, dseq, _ = x_sample.shape
    depth = w_in.shape[0]
    mp = nb * seq
    ms = db * dseq
    t_s = 8
    assert seq >= CONV_W - 1 and dseq >= CONV_W - 1 and dseq <= t_s

    x = jnp.concatenate([x_prompt.reshape(mp, d), x_sample.reshape(ms, d)], axis=0)
    xn = _rmsnorm(x, norm_mix[0], bf16)

    zeros_like_state = lambda s: jnp.zeros((nb,) + s.shape[2:], s.dtype)
    new_p = [[] for _ in range(7)]
    new_s = [[] for _ in range(7)]
    acc_s = (None, None, None)

    for l in range(depth):
        u = _proj_in(xn, _permute_w_in(w_in[l]))
        u_s = jnp.pad(u[mp:].reshape(db, dseq, P_PAD), ((0, 0), (0, t_s - dseq), (0, 0))).reshape(db * t_s, P_PAD)

        bias_row = jnp.zeros((1, LANES), f32)
        bias_row = bias_row.at[0, GL_I:GL_I + H_A].set(mlstm_i_bias[l]).at[0, GL_F:GL_F + H_A].set(mlstm_f_bias[l])
        bias_row = bias_row.at[0, GL_DT:GL_DT + H_B].set(mamba_dt_bias[l])

        par = jnp.zeros((16, MIX), f32)
        mu = rwkv_mu[l]
        for idx, val in ((PR_MU_R, mu[0:MIX]), (PR_MU_K, mu[MIX:2 * MIX]), (PR_MU_V, mu[2 * MIX:3 * MIX]),
                         (PR_W0, rwkv_w0[l]), (PR_A0, rwkv_a0[l]), (PR_KK, rwkv_k_k[l]), (PR_KA, rwkv_k_a[l]),
                         (PR_RK, rwkv_r_k[l].reshape(MIX)), (PR_LNW, rwkv_ln_w[l]), (PR_LNB, rwkv_ln_b[l])):
            par = par.at[idx].set(val)
        mu_l = jnp.pad(mu[3 * MIX:], (0, 512 - (R_W - 3 * MIX))).reshape(1, 512)
        w2p = _pad_rows(rwkv_w2[l], 0, 512)
        a2p = _pad_rows(rwkv_a2[l], LORA_W, 512)
        g2p = _pad_rows(rwkv_g2[l], LORA_W + LORA_A, 512)

        def mixers(uu, bt, t, t_valid, la, lb, big, accs, small):
            mn, mm, conv, shift = small
            ya, c_, n_, m_ = _mlstm(uu, bias_row, mlstm_norm[l], big[0], accs[0], mn, mm,
                                    bt=bt, t=t, rows=la, t_valid=t_valid)
            yb, h_ = _ssd(uu, bias_row, mamba_conv_w[l], mamba_conv_b[l], mamba_A_log[l], mamba_D[l],
                          mamba_norm[l], conv, big[1], accs[1], bt=bt, t=t, rows=lb, t_valid=t_valid)
            yc, s_ = _rwkv(uu, par, mu_l, w2p, a2p, g2p, shift, big[2], accs[2], bt=bt, t=t, t_valid=t_valid)
            return ya, yb, yc, (c_, h_, s_), (n_, m_)

        big_states = (state_mlstm_C, state_ssm, state_wkv)
        zero_big = tuple(_St(jnp.zeros((1, nb) + s.shape[2:], s.dtype), 0, 0, 1) for s in big_states)
        zero_small = tuple(zeros_like_state(s) for s in (state_mlstm_n, state_mlstm_m, state_conv, state_shift))
        ya_p, yb_p, yc_p, big_p, small_p = mixers(u, nb, seq, None, 128, 128, zero_big, (None, None, None),
                                                  zero_small)
        samp_big = tuple(_St(s, l, l, depth) for s in big_states)
        ya_s, yb_s, yc_s, acc_s, small_s = mixers(
            u_s, db, t_s, dseq, SAMPLE_ROWS, SAMPLE_ROWS, samp_big, acc_s,
            (state_mlstm_n[l], state_mlstm_m[l], state_conv[l], state_shift[l]))

        unpad = lambda y: y.reshape(db, t_s, MIX)[:, :dseq].reshape(ms, MIX).astype(bf16)
        ya = jnp.concatenate([ya_p, unpad(ya_s)], axis=0)
        yb = jnp.concatenate([yb_p, unpad(yb_s)], axis=0)
        yc = jnp.concatenate([yc_p, unpad(yc_s)], axis=0)

        last = lambda k: jnp.stack([u[b * seq + seq - k:(b + 1) * seq] for b in range(nb)])
        us3 = u[mp:].reshape(db, dseq, P_PAD)
        conv_rows = lambda z: z[:, -(CONV_W - 1):, C_XS:C_XS + CONV_DIM]
        shift_row = lambda z: jnp.concatenate([z[:, -1, C_R:C_R + 3 * MIX],
                                               z[:, -1, C_LORA:C_LORA + (R_W - 3 * MIX)]], axis=-1)
        zp = last(CONV_W - 1)
        new_p[0].append(big_p[0][0]); new_p[3].append(big_p[1][0]); new_p[5].append(big_p[2][0])
        for lst, small, z3 in ((new_p, small_p, zp), (new_s, small_s, us3)):
            lst[1].append(small[0]); lst[2].append(small[1])
            lst[4].append(conv_rows(z3)); lst[6].append(shift_row(z3))

        mixed = _branch_mix(ya, yb, yc, u, w_branch_a[l].astype(bf16), w_branch_b[l].astype(bf16),
                            w_branch_c[l].astype(bf16))
        x1, xn2 = _outproj(x, mixed, w_out[l].astype(bf16), norm_ffn[l])
        j = l // 2
        if l % 2 == 0:
            h = _ffn_up(xn2, ffn_w_gate[j], ffn_w_up[j])
            x = _ffn_down(h, ffn_w_down[j], x1)
        else:
            x = _moe(x1, xn2, norm_ffn[l], moe_router[j], moe_w_gate[j], moe_w_up[j], moe_w_down[j])
        if l + 1 < depth:
            xn = _rmsnorm(x, norm_mix[l + 1], bf16)

    y = _rmsnorm(x, norm_final, f32)
    y_prompt = y[:mp].reshape(nb, seq, d)
    y_sample = y[mp:].reshape(db, dseq, d)
    refs = (state_mlstm_C, state_mlstm_n, state_mlstm_m, state_ssm, state_conv, state_wkv, state_shift)
    outs_p = [jnp.stack(lst).astype(r.dtype) for lst, r in zip(new_p, refs)]
    new_s[0], new_s[3], new_s[5] = acc_s
    outs_s = [(v if i in (0, 3, 5) else jnp.stack(v)).astype(r.dtype) for i, (v, r) in enumerate(zip(new_s, refs))]
    return (y_prompt, y_sample, *outs_p, *outs_s)
```
